```python
import math
import jax, jax.numpy as jnp
from jax import lax
import numpy as np

D_MODEL = 4096
BATCH = 1
SEQ = 8192
DEPTH = 4

GRID_W = 64
CTX_LEN = 256
HEAD_DIM = 128
CHUNK = 128
Q_BLOCK = 128
N_HEADS_MLP = 8
N_HEADS_RET = 12
N_HEADS_DIFF = 12
W_MLP = N_HEADS_MLP * HEAD_DIM
W_RET = N_HEADS_RET * HEAD_DIM
W_DIFF = N_HEADS_DIFF * HEAD_DIM
MIX_WIDTH = W_MLP + W_RET + W_DIFF
DIFF_DIM = HEAD_DIM // 2
IN_MLP = 2 * W_MLP
IN_RET = 5 * W_RET
IN_DIFF = 3 * W_DIFF
D_IN = IN_MLP + IN_RET + IN_DIFF
RET_ROPE_BASE = 10000.0
AXIAL_ROPE_BASE = 10000.0
PEER_HEADS = 8
PEER_NKEYS = 64
PEER_EXPERTS = PEER_NKEYS * PEER_NKEYS
PEER_DQ = 256
PEER_TOPK = 16
EPS = 1e-6

kernel_name = "hymba_style_gmlp_retnet_diffattn_peer_dit"


def standardize(x):
    xf = x.astype(jnp.float32)
    mu = jnp.mean(xf, axis=-1, keepdims=True)
    var = jnp.mean(jnp.square(xf - mu), axis=-1, keepdims=True)
    return ((xf - mu) * lax.rsqrt(var + EPS)).astype(x.dtype)


def rmsnorm(x, g):
    xf = x.astype(jnp.float32)
    y = xf * lax.rsqrt(jnp.mean(xf * xf, axis=-1, keepdims=True) + EPS)
    return (y * g.astype(jnp.float32)).astype(x.dtype)


def modulate(x, g, shift, scale):
    return rmsnorm(x, g) * (1.0 + scale) + shift


def rope(x, pos, base):
    d = x.shape[-1]
    half = d // 2
    inv = base ** (-jnp.arange(half, dtype=jnp.float32) / half)
    ang = (pos.astype(jnp.float32)[:, None] * inv[None, :]).reshape((pos.shape[0],) + (1,) * (x.ndim - 3) + (half,))
    cos, sin = jnp.cos(ang).astype(x.dtype), jnp.sin(ang).astype(x.dtype)
    x1, x2 = x[..., :half], x[..., half:]
    return jnp.concatenate([x1 * cos - x2 * sin, x2 * cos + x1 * sin], axis=-1)


def axial_rope(x, row, col):
    half = x.shape[-1] // 2
    return jnp.concatenate([rope(x[..., :half], row, AXIAL_ROPE_BASE), rope(x[..., half:], col, AXIAL_ROPE_BASE)], axis=-1)


def spatial_gating(p, ln_g, ln_b, w_s, b_s):
    B, T, _ = p.shape
    u = jax.nn.gelu(p[..., :W_MLP])
    v = standardize(jax.nn.gelu(p[..., W_MLP:])) * ln_g + ln_b
    vc = v.reshape(B, T // CHUNK, CHUNK, N_HEADS_MLP, HEAD_DIM)
    mixed = jnp.einsum('hij,bnjhc->bnihc', w_s, vc) + b_s.T[None, None, :, :, None]
    return u * mixed.reshape(B, T, W_MLP)


def retention_chunks(q, k, v, log_g, s0):
    B, T, H, d = q.shape
    n = T // CHUNK
    i = jnp.arange(CHUNK, dtype=jnp.float32)
    rel = i[:, None] - i[None, :]
    lower = rel >= 0
    intra_decay = jnp.where(lower[None], jnp.exp(jnp.where(lower, rel, 0.0)[None] * log_g[:, None, None]), 0.0).astype(q.dtype)
    q_decay = jnp.exp((i[:, None] + 1.0) * log_g[None, :]).astype(q.dtype)
    k_decay = jnp.exp((CHUNK - 1.0 - i)[:, None] * log_g[None, :]).astype(q.dtype)
    chunk_decay = jnp.exp(CHUNK * log_g).astype(q.dtype)

    def step(s, qkv):
        qn, kn, vn = qkv
        scores = jnp.einsum('bihd,bjhd->bhij', qn, kn) * intra_decay
        out = jnp.einsum('bhij,bjhe->bihe', scores, vn) + jnp.einsum('bihd,bhde->bihe', qn * q_decay[..., None], s)
        s = s * chunk_decay[:, None, None] + jnp.einsum('bjhd,bjhe->bhde', kn * k_decay[..., None], vn)
        return s, out

    xs = tuple(jnp.moveaxis(a.reshape(B, n, CHUNK, H, d), 1, 0) for a in (q, k, v))
    _, out = lax.scan(step, s0, xs)
    return jnp.moveaxis(out, 0, 1).reshape(B, T, H, d)


def retention_final_state(k, v, log_g):
    T = k.shape[1]
    w = jnp.exp((T - 1.0 - jnp.arange(T, dtype=jnp.float32))[:, None] * log_g[None, :]).astype(k.dtype)
    return jnp.einsum('bthd,bthe->bhde', k * w[..., None], v)


def retention_mixer(p_ctx, p_lat, pos_ctx, pos_lat, log_omg, with_ctx_out):
    def heads(p, pos):
        B, T, _ = p.shape
        parts = [p[..., j * W_RET:(j + 1) * W_RET].reshape(B, T, N_HEADS_RET, HEAD_DIM) for j in range(5)]
        q = rope(parts[0], pos, RET_ROPE_BASE)
        k = rope(parts[1], pos, RET_ROPE_BASE) * (HEAD_DIM ** -0.5)
        return q, k, parts[2], parts[3], parts[4]

    qc, kc, vc, gfc, gbc = heads(p_ctx, pos_ctx)
    ql, kl, vl, gfl, gbl = heads(p_lat, pos_lat)
    y_lat = 0.0
    y_ctx = 0.0
    for direction, (g_lat, g_ctx) in enumerate(((gfl, gfc), (gbl, gbc))):
        log_g = jnp.log1p(-jnp.exp(log_omg[direction].astype(jnp.float32)))
        if direction == 1:
            flip = lambda a: jnp.flip(a, axis=1)
        else:
            flip = lambda a: a
        s_ctx = retention_final_state(flip(kc), flip(vc), log_g)
        o_lat = flip(retention_chunks(flip(ql), flip(kl), flip(vl), log_g, s_ctx))
        y_lat = y_lat + standardize(o_lat) * jax.nn.silu(g_lat)
        if with_ctx_out:
            o_ctx = flip(retention_chunks(flip(qc), flip(kc), flip(vc), log_g, jnp.zeros_like(s_ctx)))
            y_ctx = y_ctx + standardize(o_ctx) * jax.nn.silu(g_ctx)
    B = p_lat.shape[0]
    y_lat = y_lat.reshape(B, -1, W_RET)
    y_ctx = y_ctx.reshape(B, -1, W_RET) if with_ctx_out else None
    return y_lat, y_ctx


def diff_block(q, k, v, lam, lam_init, subln_g):
    s = jnp.einsum('bqhmd,bkhmd->bhmqk', q, k).astype(jnp.float32) * (DIFF_DIM ** -0.5)
    p = jax.nn.softmax(s, axis=-1)
    a = (p[:, :, 0] - lam * p[:, :, 1]).astype(v.dtype)
    o = jnp.einsum('bhqk,bkhe->bqhe', a, v)
    return rmsnorm(o, subln_g) * (1.0 - lam_init)


def diff_attn_mixer(p_ctx, p_lat, row, col, lam_vec, subln_g, lam_init, with_ctx_out):
    def heads(p):
        B, T, _ = p.shape
        q = p[..., :W_DIFF].reshape(B, T, N_HEADS_DIFF, 2, DIFF_DIM)
        k = p[..., W_DIFF:2 * W_DIFF].reshape(B, T, N_HEADS_DIFF, 2, DIFF_DIM)
        v = p[..., 2 * W_DIFF:].reshape(B, T, N_HEADS_DIFF, HEAD_DIM)
        return q, k, v

    qc, kc, vc = heads(p_ctx)
    ql, kl, vl = heads(p_lat)
    ql = axial_rope(ql, row, col)
    kl = axial_rope(kl, row, col)
    lf = lam_vec.astype(jnp.float32)
    lam = jnp.exp(jnp.sum(lf[0] * lf[1])) - jnp.exp(jnp.sum(lf[2] * lf[3])) + lam_init
    k_all = jnp.concatenate([kc, kl], axis=1)
    v_all = jnp.concatenate([vc, vl], axis=1)
    B, S = ql.shape[0], ql.shape[1]
    qb = jnp.moveaxis(ql.reshape(B, S // Q_BLOCK, Q_BLOCK, N_HEADS_DIFF, 2, DIFF_DIM), 1, 0)
    ob = lax.map(lambda q_blk: diff_block(q_blk, k_all, v_all, lam, lam_init, subln_g), qb)
    y_lat = jnp.moveaxis(ob, 0, 1).reshape(B, S, W_DIFF)
    y_ctx = diff_block(qc, kc, vc, lam, lam_init, subln_g).reshape(B, -1, W_DIFF) if with_ctx_out else None
    return y_lat, y_ctx


def peer_ffn(h, w_q, keys, u_tab, v_tab):
    B, T, D = h.shape
    hf = h.reshape(B * T, D)
    q = (hf @ w_q).reshape(B * T, PEER_HEADS, 2, PEER_DQ // 2)
    s = jnp.einsum('thpd,hpnd->thpn', q, keys).astype(jnp.float32)
    s_top, i_top = lax.top_k(s, PEER_TOPK)
    cand = (s_top[..., 0, :, None] + s_top[..., 1, None, :]).reshape(B * T, PEER_HEADS, PEER_TOPK * PEER_TOPK)
    cand_idx = (i_top[..., 0, :, None] * PEER_NKEYS + i_top[..., 1, None, :]).reshape(B * T, PEER_HEADS, PEER_TOPK * PEER_TOPK)
    best, pos = lax.top_k(cand, PEER_TOPK)
    expert = jnp.take_along_axis(cand_idx, pos, axis=-1).reshape(B * T, PEER_HEADS * PEER_TOPK)
    gate = jax.nn.softmax(best, axis=-1).reshape(B * T, PEER_HEADS * PEER_TOPK).astype(h.dtype)
    act = jax.nn.gelu(hf @ u_tab.T)
    w = gate * jnp.take_along_axis(act, expert, axis=1)
    mix = jnp.zeros_like(act).at[jnp.arange(B * T)[:, None], expert].add(w)
    return (mix @ v_tab).reshape(B, T, D)


def setup_inputs(seed: int = 0) -> dict:
    key = jax.random.key(seed)
    ks = jax.random.split(key, 24)
    f32 = jnp.float32

    def nrm(k, shape, scale):
        return jax.random.normal(k, shape, f32) * scale

    decay_base = (-5.0 - jnp.arange(N_HEADS_RET, dtype=f32)) * math.log(2.0)
    return {
        "x": nrm(ks[0], (BATCH, SEQ, D_MODEL), 1.0),
        "c": nrm(ks[1], (BATCH, D_MODEL), 1.0),
        "ctx": nrm(ks[2], (BATCH, CTX_LEN, D_MODEL), 1.0),
        "c_ctx": nrm(ks[3], (D_MODEL,), 1.0),
        "w_ada": nrm(ks[4], (DEPTH, D_MODEL, 6 * D_MODEL), 0.5 * D_MODEL ** -0.5),
        "b_ada": nrm(ks[5], (DEPTH, 6 * D_MODEL), 0.02),
        "norm_g": 1.0 + nrm(ks[6], (DEPTH, 2, D_MODEL), 0.02),
        "w_in": nrm(ks[7], (DEPTH, D_MODEL, D_IN), D_MODEL ** -0.5),
        "w_out": nrm(ks[8], (DEPTH, MIX_WIDTH, D_MODEL), MIX_WIDTH ** -0.5),
        "mlp_ln_g": 1.0 + nrm(ks[9], (DEPTH, W_MLP), 0.02),
        "mlp_ln_b": nrm(ks[10], (DEPTH, W_MLP), 0.02),
        "mlp_w_s": nrm(ks[11], (DEPTH, N_HEADS_MLP, CHUNK, CHUNK), CHUNK ** -0.5),
        "mlp_b_s": 1.0 + nrm(ks[12], (DEPTH, N_HEADS_MLP, CHUNK), 0.02),
        "ret_decay": decay_base + nrm(ks[13], (DEPTH, 2, N_HEADS_RET), 0.05),
        "diff_lam": nrm(ks[14], (DEPTH, 4, DIFF_DIM), 0.1),
        "diff_subln_g": 1.0 + nrm(ks[15], (DEPTH, HEAD_DIM), 0.02),
        "peer_w_q": nrm(ks[16], (DEPTH, D_MODEL, PEER_HEADS * PEER_DQ), D_MODEL ** -0.5),
        "peer_keys": nrm(ks[17], (DEPTH, PEER_HEADS, 2, PEER_NKEYS, PEER_DQ // 2), (PEER_DQ // 2) ** -0.5),
        "peer_u": nrm(ks[18], (DEPTH, PEER_EXPERTS, D_MODEL), D_MODEL ** -0.5),
        "peer_v": nrm(ks[19], (DEPTH, PEER_EXPERTS, D_MODEL), PEER_HEADS ** -0.5),
        "final_g": 1.0 + nrm(ks[20], (D_MODEL,), 0.02),
    }


def reference(x, c, ctx, c_ctx, w_ada, b_ada, norm_g, w_in, w_out, mlp_ln_g, mlp_ln_b, mlp_w_s, mlp_b_s,
              ret_decay, diff_lam, diff_subln_g, peer_w_q, peer_keys, peer_u, peer_v, final_g):
    B, S, _ = x.shape
    L = ctx.shape[1]
    ROWS = S // GRID_W
    row = jnp.repeat(jnp.arange(ROWS), GRID_W).astype(jnp.float32)
    col = jnp.tile(jnp.arange(GRID_W), ROWS).astype(jnp.float32)
    ret_pos = jnp.arange(L + S, dtype=jnp.float32)
    silu_c = jax.nn.silu(c)
    silu_cc = jax.nn.silu(c_ctx)
    ctx_s = ctx
    for l in range(DEPTH):
        ctx_out = l < DEPTH - 1
        lam_init = 0.8 - 0.6 * math.exp(-0.3 * l)
        mod = silu_c @ w_ada[l] + b_ada[l]
        mod_c = silu_cc @ w_ada[l] + b_ada[l]
        sh1, sc1, g1, sh2, sc2, g2 = jnp.split(mod[:, None, :], 6, axis=-1)
        csh1, csc1, cg1, csh2, csc2, cg2 = jnp.split(mod_c[None, None, :], 6, axis=-1)

        h = jnp.concatenate([modulate(ctx_s, norm_g[l, 0], csh1, csc1), modulate(x, norm_g[l, 0], sh1, sc1)], axis=1)
        p = h @ w_in[l]
        p_ctx, p_lat = p[:, :L], p[:, L:]
        sl_mlp = slice(0, IN_MLP)
        sl_ret = slice(IN_MLP, IN_MLP + IN_RET)
        sl_dif = slice(IN_MLP + IN_RET, D_IN)
        y_mlp_lat = spatial_gating(p_lat[..., sl_mlp], mlp_ln_g[l], mlp_ln_b[l], mlp_w_s[l], mlp_b_s[l])
        y_ret_lat, y_ret_ctx = retention_mixer(p_ctx[..., sl_ret], p_lat[..., sl_ret], ret_pos[:L], ret_pos[L:], ret_decay[l], ctx_out)
        y_dif_lat, y_dif_ctx = diff_attn_mixer(p_ctx[..., sl_dif], p_lat[..., sl_dif], row, col, diff_lam[l], diff_subln_g[l], lam_init, ctx_out)
        x = x + g1 * (jnp.concatenate([y_mlp_lat, y_ret_lat, y_dif_lat], axis=-1) @ w_out[l])

        if ctx_out:
            y_mlp_ctx = spatial_gating(p_ctx[..., sl_mlp], mlp_ln_g[l], mlp_ln_b[l], mlp_w_s[l], mlp_b_s[l])
            ctx_s = ctx_s + cg1 * (jnp.concatenate([y_mlp_ctx, y_ret_ctx, y_dif_ctx], axis=-1) @ w_out[l])
            hf = jnp.concatenate([modulate(ctx_s, norm_g[l, 1], csh2, csc2), modulate(x, norm_g[l, 1], sh2, sc2)], axis=1)
            f = peer_ffn(hf, peer_w_q[l], peer_keys[l], peer_u[l], peer_v[l])
            ctx_s = ctx_s + cg2 * f[:, :L]
            x = x + g2 * f[:, L:]
        else:
            x = x + g2 * peer_ffn(modulate(x, norm_g[l, 1], sh2, sc2), peer_w_q[l], peer_keys[l], peer_u[l], peer_v[l])
    return rmsnorm(x, final_g)
```

```python
import functools
import math

import jax
import jax.numpy as jnp
from jax import lax
from jax.experimental import pallas as pl
from jax.experimental.pallas import tpu as pltpu

F32 = jnp.float32
MXU_DTYPE = jnp.bfloat16

HEAD_DIM = 128
CHUNK = 128
GRID_W = 64
N_HEADS_MLP = 8
N_HEADS_RET = 12
N_HEADS_DIFF = 12
W_MLP = N_HEADS_MLP * HEAD_DIM
W_RET = N_HEADS_RET * HEAD_DIM
W_DIFF = N_HEADS_DIFF * HEAD_DIM
DIFF_DIM = HEAD_DIM // 2
IN_MLP = 2 * W_MLP
IN_RET = 5 * W_RET
IN_DIFF = 3 * W_DIFF
ROPE_BASE = 10000.0
PEER_HEADS = 8
PEER_NKEYS = 64
PEER_DQ = 256
PEER_TOPK = 16
EPS = 1e-6

VMEM_LIMIT_V7X = 56 * 1024 * 1024
MOD_ROWS = 8

NT_DIMS = (((1,), (1,)), ((), ()))


def _params(n_axes):
    return pltpu.CompilerParams(dimension_semantics=("arbitrary",) * n_axes,
                                vmem_limit_bytes=VMEM_LIMIT_V7X)


def _tile(n, cap, mult):
    best = None
    for t in range(mult, min(n, cap) + 1, mult):
        if n % t == 0:
            best = t
    assert best is not None, (n, cap, mult)
    return best


def _row_is_ctx(tm, n_ctx):
    row = pl.program_id(0) * tm + lax.broadcasted_iota(jnp.int32, (tm, 1), 0)
    return row < n_ctx


def _ada_kernel(c_ref, w_ref, b_ref, o_ref):
    s = jax.nn.silu(c_ref[...]).astype(MXU_DTYPE)
    o_ref[0] = jnp.dot(s, w_ref[0].astype(MXU_DTYPE), preferred_element_type=F32) + b_ref[0]


def _ada(cond, w_ada, b_ada):
    depth, d, n = w_ada.shape
    tn = _tile(n, 512, 128)
    return pl.pallas_call(
        _ada_kernel,
        out_shape=jax.ShapeDtypeStruct((depth, MOD_ROWS, n), F32),
        grid=(depth, n // tn),
        in_specs=[pl.BlockSpec((MOD_ROWS, d), lambda l, j: (0, 0)),
                  pl.BlockSpec((1, d, tn), lambda l, j: (l, 0, j)),
                  pl.BlockSpec((1, 1, tn), lambda l, j: (l, 0, j))],
        out_specs=pl.BlockSpec((1, MOD_ROWS, tn), lambda l, j: (l, 0, j)),
        compiler_params=_params(2), name="ada",
    )(cond, w_ada, b_ada.reshape(depth, 1, n))


def _modulate_kernel(x_ref, g_ref, sh_ref, sc_ref, o_ref, *, n_ctx):
    x = x_ref[...]
    y = x * lax.rsqrt(jnp.mean(x * x, axis=-1, keepdims=True) + EPS) * g_ref[...]
    is_ctx = _row_is_ctx(x.shape[0], n_ctx)
    shift = jnp.where(is_ctx, sh_ref[0:1, :], sh_ref[1:2, :])
    scale = jnp.where(is_ctx, sc_ref[0:1, :], sc_ref[1:2, :])
    o_ref[...] = (y * (1.0 + scale) + shift).astype(o_ref.dtype)


def _modulate(xs, g, mod, shift_blk, n_ctx):
    t, d = xs.shape
    tm = _tile(t, 256, 8)
    return pl.pallas_call(
        functools.partial(_modulate_kernel, n_ctx=n_ctx),
        out_shape=jax.ShapeDtypeStruct((t, d), MXU_DTYPE),
        grid=(t // tm,),
        in_specs=[pl.BlockSpec((tm, d), lambda i: (i, 0)),
                  pl.BlockSpec((1, d), lambda i: (0, 0)),
                  pl.BlockSpec((MOD_ROWS, d), lambda i: (0, shift_blk)),
                  pl.BlockSpec((MOD_ROWS, d), lambda i: (0, shift_blk + 1))],
        out_specs=pl.BlockSpec((tm, d), lambda i: (i, 0)),
        compiler_params=_params(1), name="modulate",
    )(xs, g.reshape(1, d), mod, mod)


def _mm_kernel(a_ref, b_ref, o_ref):
    o_ref[...] = jnp.dot(a_ref[...], b_ref[...], preferred_element_type=F32).astype(o_ref.dtype)


def _mm_res_kernel(a_ref, b_ref, r_ref, g_ref, o_ref, *, n_ctx):
    acc = jnp.dot(a_ref[...], b_ref[...], preferred_element_type=F32)
    gate = jnp.where(_row_is_ctx(acc.shape[0], n_ctx), g_ref[0:1, :], g_ref[1:2, :])
    o_ref[...] = r_ref[...] + gate * acc


def _mm_tiles(t, n):
    return _tile(t, 1408, 128), _tile(n, 512, 128)


def _matmul(a, b, n_off, n, out_dtype):
    t, k = a.shape
    tm, tn = _mm_tiles(t, n)
    off = n_off // tn
    assert off * tn == n_off
    return pl.pallas_call(
        _mm_kernel,
        out_shape=jax.ShapeDtypeStruct((t, n), out_dtype),
        grid=(t // tm, n // tn),
        in_specs=[pl.BlockSpec((tm, k), lambda i, j: (i, 0)),
                  pl.BlockSpec((k, tn), lambda i, j: (0, j + off))],
        out_specs=pl.BlockSpec((tm, tn), lambda i, j: (i, j)),
        compiler_params=_params(2), name="proj",
    )(a, b)


def _matmul_residual(a, b, res, mod, gate_blk, n_ctx):
    t, k = a.shape
    n = b.shape[1]
    tm, tn = _mm_tiles(t, n)
    blocks_per_gate = n // tn
    return pl.pallas_call(
        functools.partial(_mm_res_kernel, n_ctx=n_ctx),
        out_shape=jax.ShapeDtypeStruct((t, n), F32),
        grid=(t // tm, n // tn),
        in_specs=[pl.BlockSpec((tm, k), lambda i, j: (i, 0)),
                  pl.BlockSpec((k, tn), lambda i, j: (0, j)),
                  pl.BlockSpec((tm, tn), lambda i, j: (i, j)),
                  pl.BlockSpec((MOD_ROWS, tn), lambda i, j: (0, gate_blk * blocks_per_gate + j))],
        out_specs=pl.BlockSpec((tm, tn), lambda i, j: (i, j)),
        compiler_params=_params(2), name="proj_residual",
    )(a, b, res, mod)


def _prep_kernel(rq_ref, rk_ref, rv_ref, dq_ref, dk_ref, dv_ref, rc_ref, rs_ref, dc_ref, ds_ref,
                 orq_ref, ork_ref, orv_ref, odq_ref, odk_ref, odv_ref):
    rc, rs, dc, ds = rc_ref[...], rs_ref[...], dc_ref[...], ds_ref[...]
    lane = lax.broadcasted_iota(jnp.int32, rc.shape, 1)
    low16 = (lane & 16) == 0

    def rot_ret(x):
        return x * rc + pltpu.roll(x, 64, 1) * rs

    def rot_axial(x):
        partner = jnp.where(low16, pltpu.roll(x, 112, 1), pltpu.roll(x, 16, 1))
        return x * dc + partner * ds

    for h in range(N_HEADS_RET):
        sl = slice(h * HEAD_DIM, (h + 1) * HEAD_DIM)
        orq_ref[:, sl] = rot_ret(rq_ref[:, sl]).astype(MXU_DTYPE)
        ork_ref[:, sl] = (rot_ret(rk_ref[:, sl]) * (HEAD_DIM ** -0.5)).astype(MXU_DTYPE)
    for h in range(N_HEADS_DIFF):
        sl = slice(h * HEAD_DIM, (h + 1) * HEAD_DIM)
        odq_ref[:, sl] = (rot_axial(dq_ref[:, sl]) * (DIFF_DIM ** -0.5)).astype(MXU_DTYPE)
        odk_ref[:, sl] = rot_axial(dk_ref[:, sl]).astype(MXU_DTYPE)
    orv_ref[...] = rv_ref[...].astype(MXU_DTYPE)
    odv_ref[...] = dv_ref[...].astype(MXU_DTYPE)


def _prep(p_ret, p_dif, tables):
    t = p_ret.shape[0]
    tr = _tile(t, 256, 8)
    wide = lambda blk: pl.BlockSpec((tr, W_RET), lambda i: (i, blk))
    tab = pl.BlockSpec((tr, HEAD_DIM), lambda i: (i, 0))
    out = jax.ShapeDtypeStruct((t, W_RET), MXU_DTYPE)
    return pl.pallas_call(
        _prep_kernel,
        out_shape=(out,) * 6,
        grid=(t // tr,),
        in_specs=[wide(0), wide(1), wide(2), wide(0), wide(1), wide(2), tab, tab, tab, tab],
        out_specs=(wide(0),) * 6,
        compiler_params=_params(1), name="rope_prep",
    )(p_ret, p_ret, p_ret, p_dif, p_dif, p_dif, *tables)


def _rope_tables(n_ctx, seq):
    half = HEAD_DIM // 2
    inv = ROPE_BASE ** (-jnp.arange(half, dtype=F32) / half)
    ang = jnp.arange(n_ctx + seq, dtype=F32)[:, None] * inv[None, :]
    cos, sin = jnp.cos(ang), jnp.sin(ang)
    ret_cos = jnp.concatenate([cos, cos], axis=1)
    ret_sin = jnp.concatenate([-sin, sin], axis=1)

    rows = seq // GRID_W
    row = jnp.repeat(jnp.arange(rows), GRID_W).astype(F32)
    col = jnp.tile(jnp.arange(GRID_W), rows).astype(F32)
    ahalf = DIFF_DIM // 4
    ainv = ROPE_BASE ** (-jnp.arange(ahalf, dtype=F32) / ahalf)

    def group(pos):
        a = pos[:, None] * ainv[None, :]
        c, s = jnp.cos(a), jnp.sin(a)
        return jnp.concatenate([c, c], axis=1), jnp.concatenate([-s, s], axis=1)

    rc_, rs_ = group(row)
    cc_, cs_ = group(col)
    dcos = jnp.concatenate([rc_, cc_, rc_, cc_], axis=1)
    dsin = jnp.concatenate([rs_, cs_, rs_, cs_], axis=1)
    dcos = jnp.concatenate([jnp.ones((n_ctx, HEAD_DIM), F32), dcos], axis=0)
    dsin = jnp.concatenate([jnp.zeros((n_ctx, HEAD_DIM), F32), dsin], axis=0)
    return ret_cos, ret_sin, dcos, dsin


def _gating_kernel(p_ref, g_ref, b_ref, ws_ref, bs_ref, o_ref):
    u = jax.nn.gelu(p_ref[:, :W_MLP])
    gv = jax.nn.gelu(p_ref[:, W_MLP:])
    d = gv - jnp.mean(gv, axis=-1, keepdims=True)
    v = d * lax.rsqrt(jnp.mean(d * d, axis=-1, keepdims=True) + EPS) * g_ref[...] + b_ref[...]
    vb = v.astype(MXU_DTYPE)
    for h in range(N_HEADS_MLP):
        sl = slice(h * HEAD_DIM, (h + 1) * HEAD_DIM)
        mixed = jnp.dot(ws_ref[h], vb[:, sl], preferred_element_type=F32) + bs_ref[:, sl]
        o_ref[:, sl] = (u[:, sl] * mixed).astype(o_ref.dtype)


def _gating(p_mlp, ln_g, ln_b, w_s, b_s):
    t = p_mlp.shape[0]
    bias = jnp.repeat(b_s.T, HEAD_DIM, axis=1)
    return pl.pallas_call(
        _gating_kernel,
        out_shape=jax.ShapeDtypeStruct((t, W_MLP), MXU_DTYPE),
        grid=(t // CHUNK,),
        in_specs=[pl.BlockSpec((CHUNK, IN_MLP), lambda i: (i, 0)),
                  pl.BlockSpec((1, W_MLP), lambda i: (0, 0)),
                  pl.BlockSpec((1, W_MLP), lambda i: (0, 0)),
                  pl.BlockSpec((N_HEADS_MLP, CHUNK, CHUNK), lambda i: (0, 0, 0)),
                  pl.BlockSpec((CHUNK, W_MLP), lambda i: (0, 0))],
        out_specs=pl.BlockSpec((CHUNK, W_MLP), lambda i: (i, 0)),
        compiler_params=_params(1), name="spatial_gating",
    )(p_mlp, ln_g.reshape(1, W_MLP), ln_b.reshape(1, W_MLP), w_s.astype(MXU_DTYPE), bias)


def _retention_kernel(q_ref, k_ref, v_ref, g_ref, dm_ref, qd_ref, kd_ref, cd_ref, *rest, has_prev):
    if has_prev:
        prev_ref, o_ref, s_ref = rest
    else:
        o_ref, s_ref = rest

    @pl.when(pl.program_id(0) == 0)
    def _():
        s_ref[...] = jnp.zeros_like(s_ref)

    for h in range(N_HEADS_RET):
        sl = slice(h * HEAD_DIM, (h + 1) * HEAD_DIM)
        q, k, v = q_ref[:, sl], k_ref[:, sl], v_ref[:, sl]
        s = s_ref[h]
        scores = lax.dot_general(q, k, NT_DIMS, preferred_element_type=F32) * dm_ref[h]
        o = jnp.dot(scores.astype(MXU_DTYPE), v, preferred_element_type=F32)
        o = o + qd_ref[h] * jnp.dot(q, s.astype(MXU_DTYPE), preferred_element_type=F32)
        kw_t = (k.astype(F32) * kd_ref[h]).T.astype(MXU_DTYPE)
        s_ref[h] = s * cd_ref[h] + jnp.dot(kw_t, v, preferred_element_type=F32)
        d = o - jnp.mean(o, axis=-1, keepdims=True)
        y = d * lax.rsqrt(jnp.mean(d * d, axis=-1, keepdims=True) + EPS) * jax.nn.silu(g_ref[:, sl])
        if has_prev:
            y = y + prev_ref[:, sl]
        o_ref[:, sl] = y.astype(o_ref.dtype)


def _retention_tables(log_omg_dir, backward):
    log_g = jnp.log1p(-jnp.exp(log_omg_dir.astype(F32)))
    i = jnp.arange(CHUNK, dtype=F32)
    rel = i[:, None] - i[None, :]
    lower = rel >= 0
    intra = jnp.where(lower[None], jnp.exp(jnp.where(lower, rel, 0.0)[None] * log_g[:, None, None]), 0.0)
    q_decay = jnp.exp((i[None, :] + 1.0) * log_g[:, None])
    k_decay = jnp.exp((CHUNK - 1.0 - i)[None, :] * log_g[:, None])
    chunk_decay = jnp.exp(CHUNK * log_g)
    if backward:
        intra = jnp.swapaxes(intra, 1, 2)
        q_decay = q_decay[:, ::-1]
        k_decay = k_decay[:, ::-1]
    full = (N_HEADS_RET, CHUNK, HEAD_DIM)
    return (intra, jnp.broadcast_to(q_decay[:, :, None], full), jnp.broadcast_to(k_decay[:, :, None], full),
            jnp.broadcast_to(chunk_decay[:, None, None], full))


def _retention(rq, rk, rv, p_ret, log_omg_dir, n_ctx, backward, prev, out_dtype):
    t = rq.shape[0]
    n_chunks, n_ctx_chunks = t // CHUNK, n_ctx // CHUNK
    if backward:
        def chunk(s):
            return jnp.where(s < n_ctx_chunks, n_ctx_chunks - 1 - s, n_chunks + n_ctx_chunks - 1 - s)
    else:
        def chunk(s):
            return s
    gate_blk = 4 if backward else 3
    tok = pl.BlockSpec((CHUNK, W_RET), lambda s: (chunk(s), 0))
    tab = pl.BlockSpec((N_HEADS_RET, CHUNK, HEAD_DIM), lambda s: (0, 0, 0))
    in_specs = [tok, tok, tok, pl.BlockSpec((CHUNK, W_RET), lambda s: (chunk(s), gate_blk)), tab, tab, tab, tab]
    args = [rq, rk, rv, p_ret, *_retention_tables(log_omg_dir, backward)]
    if prev is not None:
        in_specs.append(tok)
        args.append(prev)
    return pl.pallas_call(
        functools.partial(_retention_kernel, has_prev=prev is not None),
        out_shape=jax.ShapeDtypeStruct((t, W_RET), out_dtype),
        grid=(n_chunks,),
        in_specs=in_specs,
        out_specs=tok,
        scratch_shapes=[pltpu.VMEM((N_HEADS_RET, HEAD_DIM, HEAD_DIM), F32)],
        compiler_params=_params(1), name="retention",
    )(*args)


def _diff_attn_kernel(lam_ref, g_ref, q_ref, k_ref, v_ref, o_ref, *, tk, n_kv, lam_init):
    tq = q_ref.shape[0]
    q = q_ref[...].astype(F32)
    lane = lax.broadcasted_iota(jnp.int32, q.shape, 1)
    qs = jnp.concatenate([jnp.where(lane < DIFF_DIM, q, 0.0), jnp.where(lane >= DIFF_DIM, q, 0.0)],
                         axis=0).astype(MXU_DTYPE)

    def body(j, carry):
        m, l, acc = carry
        start = pl.multiple_of(j * tk, tk)
        kc = k_ref[pl.ds(start, tk), :]
        vc = v_ref[pl.ds(start, tk), :]
        s = lax.dot_general(qs, kc, NT_DIMS, preferred_element_type=F32)
        m_new = jnp.maximum(m, jnp.max(s, axis=-1, keepdims=True))
        alpha = jnp.exp(m - m_new)
        p = jnp.exp(s - m_new)
        l = alpha * l + jnp.sum(p, axis=-1, keepdims=True)
        acc = alpha * acc + jnp.dot(p.astype(MXU_DTYPE), vc, preferred_element_type=F32)
        return m_new, l, acc

    init = (jnp.full((2 * tq, 1), -jnp.inf, F32), jnp.zeros((2 * tq, 1), F32), jnp.zeros((2 * tq, HEAD_DIM), F32))
    _, l, acc = lax.fori_loop(0, n_kv, body, init)
    o = acc / l
    lf = lam_ref[...]
    lam = (jnp.exp(jnp.sum(lf[0:1] * lf[1:2], axis=-1, keepdims=True))
           - jnp.exp(jnp.sum(lf[2:3] * lf[3:4], axis=-1, keepdims=True)) + lam_init)
    d = o[:tq] - lam * o[tq:]
    y = d * lax.rsqrt(jnp.mean(d * d, axis=-1, keepdims=True) + EPS) * g_ref[...]
    o_ref[...] = (y * (1.0 - lam_init)).astype(o_ref.dtype)


def _diff_attn(dq, dk, dv, lam_vec, subln_g, lam_init, q_start, n_q, kv_len):
    tq = _tile(n_q, 256, 8)
    tk = _tile(kv_len, 768, 128)
    q_off = q_start // tq
    assert q_off * tq == q_start
    return pl.pallas_call(
        functools.partial(_diff_attn_kernel, tk=tk, n_kv=kv_len // tk, lam_init=lam_init),
        out_shape=jax.ShapeDtypeStruct((n_q, W_DIFF), MXU_DTYPE),
        grid=(N_HEADS_DIFF, n_q // tq),
        in_specs=[pl.BlockSpec((4, DIFF_DIM), lambda h, i: (0, 0)),
                  pl.BlockSpec((1, HEAD_DIM), lambda h, i: (0, 0)),
                  pl.BlockSpec((tq, HEAD_DIM), lambda h, i: (i + q_off, h)),
                  pl.BlockSpec((kv_len, HEAD_DIM), lambda h, i: (0, h)),
                  pl.BlockSpec((kv_len, HEAD_DIM), lambda h, i: (0, h))],
        out_specs=pl.BlockSpec((tq, HEAD_DIM), lambda h, i: (i, h)),
        compiler_params=_params(2), name="diff_attention",
    )(lam_vec, subln_g.reshape(1, HEAD_DIM), dq, dk, dv)


_CAND_ROWS = PEER_TOPK + 8 * 7 + 8


def _top16_rows(x, out_ref):
    cur = x
    for r in range(PEER_TOPK):
        m = jnp.max(cur, axis=0, keepdims=True)
        out_ref[r:r + 1, :] = m
        if r + 1 < PEER_TOPK:
            cur = jnp.where(cur == m, -jnp.inf, cur)


def _route_kernel(q_ref, keys_ref, a_ref, b_ref, ea_ref, eb_ref, tau_ref, a16_ref, b16_ref, cand_ref):
    tt = q_ref.shape[0]
    sub = lax.broadcasted_iota(jnp.int32, (8, tt), 0)
    for h in range(PEER_HEADS):
        half = PEER_DQ // 2
        qa = q_ref[:, (2 * h) * half:(2 * h + 1) * half]
        qb = q_ref[:, (2 * h + 1) * half:(2 * h + 2) * half]
        a = lax.dot_general(keys_ref[2 * h], qa, NT_DIMS, preferred_element_type=F32)
        b = lax.dot_general(keys_ref[2 * h + 1], qb, NT_DIMS, preferred_element_type=F32)
        _top16_rows(a, a16_ref)
        _top16_rows(b, b16_ref)
        cand_ref[0:PEER_TOPK, :] = a16_ref[0:1, :] + b16_ref[...]
        for i in range(1, 8):
            c = a16_ref[i:i + 1, :] + b16_ref[0:8, :]
            cand_ref[8 + 8 * i:16 + 8 * i, :] = jnp.where(sub < PEER_TOPK // (i + 1), c, -jnp.inf)
        cand_ref[_CAND_ROWS - 8:_CAND_ROWS, :] = a16_ref[8:16, :] + b16_ref[0:1, :]
        cand = cand_ref[...]
        cur = cand
        for r in range(PEER_TOPK):
            tau = jnp.max(cur, axis=0, keepdims=True)
            if r + 1 < PEER_TOPK:
                cur = jnp.where(cur == tau, -jnp.inf, cur)
        a_max, b_max = a16_ref[0:1, :], b16_ref[0:1, :]
        z = jnp.sum(jnp.where(cand >= tau, jnp.exp(cand - (a_max + b_max)), 0.0), axis=0, keepdims=True)
        a_ref[h] = a
        b_ref[h] = b
        ea_ref[h] = jnp.exp(a - a_max)
        eb_ref[h] = jnp.exp(b - b_max) / z
        tau_ref[h] = tau


def _route(q, keys):
    t = q.shape[0]
    tt = _tile(t, 256, 128)
    grid_out = jax.ShapeDtypeStruct((PEER_HEADS, PEER_NKEYS, t), F32)
    big = pl.BlockSpec((PEER_HEADS, PEER_NKEYS, tt), lambda i: (0, 0, i))
    return pl.pallas_call(
        _route_kernel,
        out_shape=(grid_out, grid_out, grid_out, grid_out, jax.ShapeDtypeStruct((PEER_HEADS, 1, t), F32)),
        grid=(t // tt,),
        in_specs=[pl.BlockSpec((tt, PEER_HEADS * PEER_DQ), lambda i: (i, 0)),
                  pl.BlockSpec((2 * PEER_HEADS, PEER_NKEYS, PEER_DQ // 2), lambda i: (0, 0, 0))],
        out_specs=(big, big, big, big, pl.BlockSpec((PEER_HEADS, 1, tt), lambda i: (0, 0, i))),
        scratch_shapes=[pltpu.VMEM((PEER_TOPK, tt), F32), pltpu.VMEM((PEER_TOPK, tt), F32),
                        pltpu.VMEM((_CAND_ROWS, tt), F32)],
        compiler_params=_params(1), name="peer_route",
    )(q, keys.reshape(2 * PEER_HEADS, PEER_NKEYS, PEER_DQ // 2).astype(MXU_DTYPE))


def _mix_kernel(h_ref, u_ref, a_ref, ea_ref, b_ref, eb_ref, tau_ref, o_ref, gt_ref):
    act = jax.nn.gelu(lax.dot_general(h_ref[...], u_ref[...], NT_DIMS, preferred_element_type=F32))
    n_i = a_ref.shape[1]
    for ii in range(n_i):
        acc = None
        for h in range(PEER_HEADS):
            picked = (a_ref[h, ii:ii + 1, :] + b_ref[h]) >= tau_ref[h]
            g = jnp.where(picked, ea_ref[h, ii:ii + 1, :] * eb_ref[h], 0.0)
            acc = g if acc is None else acc + g
        gt_ref[ii * PEER_NKEYS:(ii + 1) * PEER_NKEYS, :] = acc
    o_ref[...] = (act * gt_ref[...].T).astype(o_ref.dtype)


def _mix(hf, u_tab, a, ea, b, eb, tau):
    t, d = hf.shape
    n_exp = u_tab.shape[0]
    tt = _tile(t, 768, 128)
    te = _tile(n_exp, 512, 8 * PEER_NKEYS)
    n_i = te // PEER_NKEYS
    part = pl.BlockSpec((PEER_HEADS, n_i, tt), lambda i, e: (0, e, i))
    full = pl.BlockSpec((PEER_HEADS, PEER_NKEYS, tt), lambda i, e: (0, 0, i))
    return pl.pallas_call(
        _mix_kernel,
        out_shape=jax.ShapeDtypeStruct((t, n_exp), MXU_DTYPE),
        grid=(t // tt, n_exp // te),
        in_specs=[pl.BlockSpec((tt, d), lambda i, e: (i, 0)),
                  pl.BlockSpec((te, d), lambda i, e: (e, 0)),
                  part, part, full, full,
                  pl.BlockSpec((PEER_HEADS, 1, tt), lambda i, e: (0, 0, i))],
        out_specs=pl.BlockSpec((tt, te), lambda i, e: (i, e)),
        scratch_shapes=[pltpu.VMEM((te, tt), F32)],
        compiler_params=_params(2), name="peer_mix",
    )(hf, u_tab, a, ea, b, eb, tau)


def _final_kernel(x_ref, g_ref, o_ref):
    x = x_ref[...]
    o_ref[...] = x * lax.rsqrt(jnp.mean(x * x, axis=-1, keepdims=True) + EPS) * g_ref[...]


def _final_norm(xs, g, n_ctx):
    t, d = xs.shape
    seq = t - n_ctx
    tm = _tile(math.gcd(seq, n_ctx), 256, 8)
    off = n_ctx // tm
    return pl.pallas_call(
        _final_kernel,
        out_shape=jax.ShapeDtypeStruct((seq, d), F32),
        grid=(seq // tm,),
        in_specs=[pl.BlockSpec((tm, d), lambda i: (i + off, 0)), pl.BlockSpec((1, d), lambda i: (0, 0))],
        out_specs=pl.BlockSpec((tm, d), lambda i: (i, 0)),
        compiler_params=_params(1), name="final_norm",
    )(xs, g.reshape(1, d))


def kernel(x, c, ctx, c_ctx, w_ada, b_ada, norm_g, w_in, w_out, mlp_ln_g, mlp_ln_b, mlp_w_s, mlp_b_s,
           ret_decay, diff_lam, diff_subln_g, peer_w_q, peer_keys, peer_u, peer_v, final_g):
    batch, seq, d = x.shape
    n_ctx = ctx.shape[1]
    depth = w_in.shape[0]
    assert batch == 1 and seq % CHUNK == 0 and n_ctx % CHUNK == 0 and seq % GRID_W == 0
    t = n_ctx + seq

    xs = jnp.concatenate([ctx[0], x[0]], axis=0)
    cond = jnp.zeros((MOD_ROWS, d), F32).at[0].set(c_ctx).at[1].set(c[0])
    mod_all = _ada(cond, w_ada, b_ada)
    tables = _rope_tables(n_ctx, seq)

    for l in range(depth):
        ctx_out = l < depth - 1
        lam_init = 0.8 - 0.6 * math.exp(-0.3 * l)
        mod = mod_all[l]

        h = _modulate(xs, norm_g[l, 0], mod, 0, n_ctx)
        w_in_l = w_in[l].astype(MXU_DTYPE)
        p_mlp = _matmul(h, w_in_l, 0, IN_MLP, F32)
        p_ret = _matmul(h, w_in_l, IN_MLP, IN_RET, F32)
        p_dif = _matmul(h, w_in_l, IN_MLP + IN_RET, IN_DIFF, F32)

        y_mlp = _gating(p_mlp, mlp_ln_g[l], mlp_ln_b[l], mlp_w_s[l], mlp_b_s[l])
        rq, rk, rv, dq, dk, dv = _prep(p_ret, p_dif, tables)
        y_fwd = _retention(rq, rk, rv, p_ret, ret_decay[l, 0], n_ctx, False, None, F32)
        y_ret = _retention(rq, rk, rv, p_ret, ret_decay[l, 1], n_ctx, True, y_fwd, MXU_DTYPE)
        y_dif = _diff_attn(dq, dk, dv, diff_lam[l], diff_subln_g[l], lam_init, n_ctx, seq, t)
        if ctx_out:
            y_dif_ctx = _diff_attn(dq, dk, dv, diff_lam[l], diff_subln_g[l], lam_init, 0, n_ctx, n_ctx)
        else:
            y_dif_ctx = jnp.zeros((n_ctx, W_DIFF), MXU_DTYPE)
        y = jnp.concatenate([y_mlp, y_ret, jnp.concatenate([y_dif_ctx, y_dif], axis=0)], axis=1)
        xs = _matmul_residual(y, w_out[l].astype(MXU_DTYPE), xs, mod, 2, n_ctx)

        hf = _modulate(xs, norm_g[l, 1], mod, 3, n_ctx)
        q = _matmul(hf, peer_w_q[l].astype(MXU_DTYPE), 0, PEER_HEADS * PEER_DQ, MXU_DTYPE)
        a, b, ea, eb, tau = _route(q, peer_keys[l])
        mix = _mix(hf, peer_u[l].astype(MXU_DTYPE), a, ea, b, eb, tau)
        xs = _matmul_residual(mix, peer_v[l].astype(MXU_DTYPE), xs, mod, 5, n_ctx)

    return _final_norm(xs, final_g, n_ctx)[None]
```

```python
import functools
import math

import jax
import jax.numpy as jnp
from jax import lax
from jax.experimental import pallas as pl
from jax.experimental.pallas import tpu as pltpu

F32 = jnp.float32
MXU_DTYPE = jnp.bfloat16

HEAD_DIM = 128
CHUNK = 128
GRID_W = 64
N_HEADS_MLP = 8
N_HEADS_RET = 12
N_HEADS_DIFF = 12
W_MLP = N_HEADS_MLP * HEAD_DIM
W_RET = N_HEADS_RET * HEAD_DIM
W_DIFF = N_HEADS_DIFF * HEAD_DIM
DIFF_DIM = HEAD_DIM // 2
IN_MLP = 2 * W_MLP
IN_RET = 5 * W_RET
IN_DIFF = 3 * W_DIFF
ROPE_BASE = 10000.0
PEER_HEADS = 8
PEER_NKEYS = 64
PEER_DQ = 256
PEER_TOPK = 16
EPS = 1e-6
LOG2_E = math.log2(math.e)

VMEM_LIMIT_V7X = 56 * 1024 * 1024
MOD_ROWS = 8

NT_DIMS = (((1,), (1,)), ((), ()))


def _params(n_axes):
    return pltpu.CompilerParams(dimension_semantics=("arbitrary",) * n_axes,
                                vmem_limit_bytes=VMEM_LIMIT_V7X)


def _tile(n, cap, mult):
    best = None
    for t in range(mult, min(n, cap) + 1, mult):
        if n % t == 0:
            best = t
    assert best is not None, (n, cap, mult)
    return best


def _row_is_ctx(tm, n_ctx):
    row = pl.program_id(0) * tm + lax.broadcasted_iota(jnp.int32, (tm, 1), 0)
    return row < n_ctx


def _ada_kernel(c_ref, w_ref, b_ref, o_ref):
    s = jax.nn.silu(c_ref[...]).astype(MXU_DTYPE)
    o_ref[0] = jnp.dot(s, w_ref[0].astype(MXU_DTYPE), preferred_element_type=F32) + b_ref[0]


def _ada(cond, w_ada, b_ada):
    depth, d, n = w_ada.shape
    tn = _tile(n, 512, 128)
    return pl.pallas_call(
        _ada_kernel,
        out_shape=jax.ShapeDtypeStruct((depth, MOD_ROWS, n), F32),
        grid=(depth, n // tn),
        in_specs=[pl.BlockSpec((MOD_ROWS, d), lambda l, j: (0, 0)),
                  pl.BlockSpec((1, d, tn), lambda l, j: (l, 0, j)),
                  pl.BlockSpec((1, 1, tn), lambda l, j: (l, 0, j))],
        out_specs=pl.BlockSpec((1, MOD_ROWS, tn), lambda l, j: (l, 0, j)),
        compiler_params=_params(2), name="ada",
    )(cond, w_ada, b_ada.reshape(depth, 1, n))


def _modulate_kernel(x_ref, g_ref, sh_ref, sc_ref, o_ref, *, n_ctx):
    x = x_ref[...]
    y = x * lax.rsqrt(jnp.mean(x * x, axis=-1, keepdims=True) + EPS) * g_ref[...]
    is_ctx = _row_is_ctx(x.shape[0], n_ctx)
    shift = jnp.where(is_ctx, sh_ref[0:1, :], sh_ref[1:2, :])
    scale = jnp.where(is_ctx, sc_ref[0:1, :], sc_ref[1:2, :])
    o_ref[...] = (y * (1.0 + scale) + shift).astype(o_ref.dtype)


def _modulate(xs, g, mod, shift_blk, n_ctx):
    t, d = xs.shape
    tm = _tile(t, 256, 8)
    return pl.pallas_call(
        functools.partial(_modulate_kernel, n_ctx=n_ctx),
        out_shape=jax.ShapeDtypeStruct((t, d), MXU_DTYPE),
        grid=(t // tm,),
        in_specs=[pl.BlockSpec((tm, d), lambda i: (i, 0)),
                  pl.BlockSpec((1, d), lambda i: (0, 0)),
                  pl.BlockSpec((MOD_ROWS, d), lambda i: (0, shift_blk)),
                  pl.BlockSpec((MOD_ROWS, d), lambda i: (0, shift_blk + 1))],
        out_specs=pl.BlockSpec((tm, d), lambda i: (i, 0)),
        compiler_params=_params(1), name="modulate",
    )(xs, g.reshape(1, d), mod, mod)


def _mm_kernel(a_ref, b_ref, o_ref):
    o_ref[...] = jnp.dot(a_ref[...], b_ref[...], preferred_element_type=F32).astype(o_ref.dtype)


def _mm_res_kernel(*refs, n_parts, n_ctx):
    a_refs, b_refs = refs[:n_parts], refs[n_parts:2 * n_parts]
    r_ref, g_ref, o_ref = refs[2 * n_parts:]
    acc = None
    for a_ref, b_ref in zip(a_refs, b_refs):
        part = jnp.dot(a_ref[...], b_ref[...], preferred_element_type=F32)
        acc = part if acc is None else acc + part
    gate = jnp.where(_row_is_ctx(acc.shape[0], n_ctx), g_ref[0:1, :], g_ref[1:2, :])
    o_ref[...] = r_ref[...] + gate * acc


def _mm_tiles(t, n):
    return _tile(t, 1408, 128), _tile(n, 512, 128)


def _matmul(a, b, n_off, n, out_dtype):
    t, k = a.shape
    tm, tn = _mm_tiles(t, n)
    off = n_off // tn
    assert off * tn == n_off
    return pl.pallas_call(
        _mm_kernel,
        out_shape=jax.ShapeDtypeStruct((t, n), out_dtype),
        grid=(t // tm, n // tn),
        in_specs=[pl.BlockSpec((tm, k), lambda i, j: (i, 0)),
                  pl.BlockSpec((k, tn), lambda i, j: (0, j + off))],
        out_specs=pl.BlockSpec((tm, tn), lambda i, j: (i, j)),
        compiler_params=_params(2), name="proj",
    )(a, b)


def _matmul_residual(a_parts, b_parts, res, mod, gate_blk, n_ctx):
    t, n = res.shape
    tm, tn = _mm_tiles(t, n)
    blocks_per_gate = n // tn
    a_specs = [pl.BlockSpec((tm, a.shape[1]), lambda i, j: (i, 0)) for a in a_parts]
    b_specs = [pl.BlockSpec((b.shape[0], tn), lambda i, j: (0, j)) for b in b_parts]
    return pl.pallas_call(
        functools.partial(_mm_res_kernel, n_parts=len(a_parts), n_ctx=n_ctx),
        out_shape=jax.ShapeDtypeStruct((t, n), F32),
        grid=(t // tm, n // tn),
        in_specs=[*a_specs, *b_specs,
                  pl.BlockSpec((tm, tn), lambda i, j: (i, j)),
                  pl.BlockSpec((MOD_ROWS, tn), lambda i, j: (0, gate_blk * blocks_per_gate + j))],
        out_specs=pl.BlockSpec((tm, tn), lambda i, j: (i, j)),
        compiler_params=_params(2), name="proj_residual",
    )(*a_parts, *b_parts, res, mod)


def _prep_kernel(rq_ref, rk_ref, rv_ref, dq_ref, dk_ref, dv_ref, rc_ref, rs_ref, dc_ref, ds_ref,
                 orq_ref, ork_ref, orv_ref, odq_ref, odk_ref, odv_ref):
    rc, rs, dc, ds = rc_ref[...], rs_ref[...], dc_ref[...], ds_ref[...]
    lane = lax.broadcasted_iota(jnp.int32, rc.shape, 1)
    low16 = (lane & 16) == 0

    def rot_ret(x):
        return x * rc + pltpu.roll(x, 64, 1) * rs

    def rot_axial(x):
        partner = jnp.where(low16, pltpu.roll(x, 112, 1), pltpu.roll(x, 16, 1))
        return x * dc + partner * ds

    for h in range(N_HEADS_RET):
        sl = slice(h * HEAD_DIM, (h + 1) * HEAD_DIM)
        orq_ref[:, sl] = rot_ret(rq_ref[:, sl]).astype(MXU_DTYPE)
        ork_ref[:, sl] = (rot_ret(rk_ref[:, sl]) * (HEAD_DIM ** -0.5)).astype(MXU_DTYPE)
    for h in range(N_HEADS_DIFF):
        sl = slice(h * HEAD_DIM, (h + 1) * HEAD_DIM)
        odq_ref[:, sl] = (rot_axial(dq_ref[:, sl]) * (DIFF_DIM ** -0.5 * LOG2_E)).astype(MXU_DTYPE)
        odk_ref[:, sl] = rot_axial(dk_ref[:, sl]).astype(MXU_DTYPE)
    orv_ref[...] = rv_ref[...].astype(MXU_DTYPE)
    odv_ref[...] = dv_ref[...].astype(MXU_DTYPE)


def _prep(p_ret, p_dif, tables):
    t = p_ret.shape[0]
    tr = _tile(t, 256, 8)
    wide = lambda blk: pl.BlockSpec((tr, W_RET), lambda i: (i, blk))
    tab = pl.BlockSpec((tr, HEAD_DIM), lambda i: (i, 0))
    out = jax.ShapeDtypeStruct((t, W_RET), MXU_DTYPE)
    return pl.pallas_call(
        _prep_kernel,
        out_shape=(out,) * 6,
        grid=(t // tr,),
        in_specs=[wide(0), wide(1), wide(2), wide(0), wide(1), wide(2), tab, tab, tab, tab],
        out_specs=(wide(0),) * 6,
        compiler_params=_params(1), name="rope_prep",
    )(p_ret, p_ret, p_ret, p_dif, p_dif, p_dif, *tables)


def _rope_tables(n_ctx, seq):
    half = HEAD_DIM // 2
    inv = ROPE_BASE ** (-jnp.arange(half, dtype=F32) / half)
    ang = jnp.arange(n_ctx + seq, dtype=F32)[:, None] * inv[None, :]
    cos, sin = jnp.cos(ang), jnp.sin(ang)
    ret_cos = jnp.concatenate([cos, cos], axis=1)
    ret_sin = jnp.concatenate([-sin, sin], axis=1)

    rows = seq // GRID_W
    row = jnp.repeat(jnp.arange(rows), GRID_W).astype(F32)
    col = jnp.tile(jnp.arange(GRID_W), rows).astype(F32)
    ahalf = DIFF_DIM // 4
    ainv = ROPE_BASE ** (-jnp.arange(ahalf, dtype=F32) / ahalf)

    def group(pos):
        a = pos[:, None] * ainv[None, :]
        c, s = jnp.cos(a), jnp.sin(a)
        return jnp.concatenate([c, c], axis=1), jnp.concatenate([-s, s], axis=1)

    rc_, rs_ = group(row)
    cc_, cs_ = group(col)
    dcos = jnp.concatenate([rc_, cc_, rc_, cc_], axis=1)
    dsin = jnp.concatenate([rs_, cs_, rs_, cs_], axis=1)
    dcos = jnp.concatenate([jnp.ones((n_ctx, HEAD_DIM), F32), dcos], axis=0)
    dsin = jnp.concatenate([jnp.zeros((n_ctx, HEAD_DIM), F32), dsin], axis=0)
    return ret_cos, ret_sin, dcos, dsin


def _gating_kernel(p_ref, g_ref, b_ref, ws_ref, bs_ref, o_ref):
    u = jax.nn.gelu(p_ref[:, :W_MLP])
    gv = jax.nn.gelu(p_ref[:, W_MLP:])
    d = gv - jnp.mean(gv, axis=-1, keepdims=True)
    v = d * lax.rsqrt(jnp.mean(d * d, axis=-1, keepdims=True) + EPS) * g_ref[...] + b_ref[...]
    vb = v.astype(MXU_DTYPE)
    for h in range(N_HEADS_MLP):
        sl = slice(h * HEAD_DIM, (h + 1) * HEAD_DIM)
        mixed = jnp.dot(ws_ref[h], vb[:, sl], preferred_element_type=F32) + bs_ref[:, sl]
        o_ref[:, sl] = (u[:, sl] * mixed).astype(o_ref.dtype)


def _gating(p_mlp, ln_g, ln_b, w_s, b_s):
    t = p_mlp.shape[0]
    bias = jnp.repeat(b_s.T, HEAD_DIM, axis=1)
    return pl.pallas_call(
        _gating_kernel,
        out_shape=jax.ShapeDtypeStruct((t, W_MLP), MXU_DTYPE),
        grid=(t // CHUNK,),
        in_specs=[pl.BlockSpec((CHUNK, IN_MLP), lambda i: (i, 0)),
                  pl.BlockSpec((1, W_MLP), lambda i: (0, 0)),
                  pl.BlockSpec((1, W_MLP), lambda i: (0, 0)),
                  pl.BlockSpec((N_HEADS_MLP, CHUNK, CHUNK), lambda i: (0, 0, 0)),
                  pl.BlockSpec((CHUNK, W_MLP), lambda i: (0, 0))],
        out_specs=pl.BlockSpec((CHUNK, W_MLP), lambda i: (i, 0)),
        compiler_params=_params(1), name="spatial_gating",
    )(p_mlp, ln_g.reshape(1, W_MLP), ln_b.reshape(1, W_MLP), w_s.astype(MXU_DTYPE), bias)


def _retention_kernel(q_ref, k_ref, v_ref, g_ref, dm_ref, qd_ref, kd_ref, cd_ref, *rest, has_prev):
    if has_prev:
        prev_ref, o_ref, s_ref = rest
    else:
        o_ref, s_ref = rest

    @pl.when(pl.program_id(0) == 0)
    def _():
        s_ref[...] = jnp.zeros_like(s_ref)

    for h in range(N_HEADS_RET):
        sl = slice(h * HEAD_DIM, (h + 1) * HEAD_DIM)
        q, k, v = q_ref[:, sl], k_ref[:, sl], v_ref[:, sl]
        s = s_ref[h]
        scores = lax.dot_general(q, k, NT_DIMS, preferred_element_type=F32) * dm_ref[h]
        o = jnp.dot(scores.astype(MXU_DTYPE), v, preferred_element_type=F32)
        o = o + qd_ref[h] * jnp.dot(q, s.astype(MXU_DTYPE), preferred_element_type=F32)
        kw_t = (k.astype(F32) * kd_ref[h]).T.astype(MXU_DTYPE)
        s_ref[h] = s * cd_ref[h] + jnp.dot(kw_t, v, preferred_element_type=F32)
        d = o - jnp.mean(o, axis=-1, keepdims=True)
        y = d * lax.rsqrt(jnp.mean(d * d, axis=-1, keepdims=True) + EPS) * jax.nn.silu(g_ref[:, sl])
        if has_prev:
            y = y + prev_ref[:, sl]
        o_ref[:, sl] = y.astype(o_ref.dtype)


def _retention_tables(log_omg_dir, backward):
    log_g = jnp.log1p(-jnp.exp(log_omg_dir.astype(F32)))
    i = jnp.arange(CHUNK, dtype=F32)
    rel = i[:, None] - i[None, :]
    lower = rel >= 0
    intra = jnp.where(lower[None], jnp.exp(jnp.where(lower, rel, 0.0)[None] * log_g[:, None, None]), 0.0)
    q_decay = jnp.exp((i[None, :] + 1.0) * log_g[:, None])
    k_decay = jnp.exp((CHUNK - 1.0 - i)[None, :] * log_g[:, None])
    chunk_decay = jnp.exp(CHUNK * log_g)
    if backward:
        intra = jnp.swapaxes(intra, 1, 2)
        q_decay = q_decay[:, ::-1]
        k_decay = k_decay[:, ::-1]
    full = (N_HEADS_RET, CHUNK, HEAD_DIM)
    return (intra, jnp.broadcast_to(q_decay[:, :, None], full), jnp.broadcast_to(k_decay[:, :, None], full),
            jnp.broadcast_to(chunk_decay[:, None, None], full))


def _retention(rq, rk, rv, p_ret, log_omg_dir, n_ctx, backward, prev, out_dtype):
    t = rq.shape[0]
    n_chunks, n_ctx_chunks = t // CHUNK, n_ctx // CHUNK
    if backward:
        def chunk(s):
            return jnp.where(s < n_ctx_chunks, n_ctx_chunks - 1 - s, n_chunks + n_ctx_chunks - 1 - s)
    else:
        def chunk(s):
            return s
    gate_blk = 4 if backward else 3
    tok = pl.BlockSpec((CHUNK, W_RET), lambda s: (chunk(s), 0))
    tab = pl.BlockSpec((N_HEADS_RET, CHUNK, HEAD_DIM), lambda s: (0, 0, 0))
    in_specs = [tok, tok, tok, pl.BlockSpec((CHUNK, W_RET), lambda s: (chunk(s), gate_blk)), tab, tab, tab, tab]
    args = [rq, rk, rv, p_ret, *_retention_tables(log_omg_dir, backward)]
    if prev is not None:
        in_specs.append(tok)
        args.append(prev)
    return pl.pallas_call(
        functools.partial(_retention_kernel, has_prev=prev is not None),
        out_shape=jax.ShapeDtypeStruct((t, W_RET), out_dtype),
        grid=(n_chunks,),
        in_specs=in_specs,
        out_specs=tok,
        scratch_shapes=[pltpu.VMEM((N_HEADS_RET, HEAD_DIM, HEAD_DIM), F32)],
        compiler_params=_params(1), name="retention",
    )(*args)


def _diff_attn_kernel(lam_ref, g_ref, *refs, n_grp, tk, n_kv, lam_init):
    q_refs = refs[:n_grp]
    k_ref, v_ref, o_ref, qs_ref, s0_ref, s1_ref, acc_ref = refs[n_grp:]
    tq = q_refs[0].shape[0]
    lane = lax.broadcasted_iota(jnp.int32, (tq, HEAD_DIM), 1)
    for g in range(n_grp):
        q = q_refs[g][...].astype(F32)
        qs_ref[g, 0:tq, :] = jnp.where(lane < DIFF_DIM, q, 0.0).astype(MXU_DTYPE)
        qs_ref[g, tq:2 * tq, :] = jnp.where(lane >= DIFF_DIM, q, 0.0).astype(MXU_DTYPE)
    ones_col = jnp.where(lax.broadcasted_iota(jnp.int32, (tk, HEAD_DIM), 1) == 0, 1.0, 0.0).astype(MXU_DTYPE)
    lf = lam_ref[...]
    lam = (jnp.exp(jnp.sum(lf[0:1] * lf[1:2], axis=-1, keepdims=True))
           - jnp.exp(jnp.sum(lf[2:3] * lf[3:4], axis=-1, keepdims=True)) + lam_init)

    def scores(g, j, s_ref):
        start = pl.multiple_of(j * tk, tk)
        s_ref[...] = lax.dot_general(qs_ref[g], k_ref[pl.ds(start, tk), :], NT_DIMS, preferred_element_type=F32)

    def consume(j, s_ref, m):
        s = s_ref[...]
        m_new = jnp.maximum(m, jnp.max(s, axis=-1, keepdims=True))
        p = jnp.exp2(s - m_new).astype(MXU_DTYPE)
        start = pl.multiple_of(j * tk, tk)
        v_ext = jnp.concatenate([v_ref[pl.ds(start, tk), :], ones_col], axis=1)
        acc_ref[...] = jnp.exp2(m - m_new) * acc_ref[...] + jnp.dot(p, v_ext, preferred_element_type=F32)
        return m_new

    def finalize(g):
        acc = acc_ref[...]
        o = acc[:, :HEAD_DIM] / acc[:, HEAD_DIM:HEAD_DIM + 1]
        d = o[:tq] - lam * o[tq:]
        y = d * lax.rsqrt(jnp.mean(d * d, axis=-1, keepdims=True) + EPS) * g_ref[...]
        o_ref[g * tq:(g + 1) * tq, :] = (y * (1.0 - lam_init)).astype(o_ref.dtype)

    n_pairs = (n_kv - 1) // 2
    cur, nxt = s0_ref, s1_ref
    scores(0, 0, cur)
    for g in range(n_grp):
        acc_ref[...] = jnp.zeros_like(acc_ref)

        def pair(jj, m, g=g, cur=cur, nxt=nxt):
            j = 2 * jj
            scores(g, j + 1, nxt)
            m = consume(j, cur, m)
            scores(g, j + 2, cur)
            return consume(j + 1, nxt, m)

        m = lax.fori_loop(0, n_pairs, pair, jnp.full((2 * tq, 1), -jnp.inf, F32))
        more = g + 1 < n_grp
        if n_kv % 2 == 1:
            if more:
                scores(g + 1, 0, nxt)
            consume(n_kv - 1, cur, m)
            cur, nxt = nxt, cur
        else:
            scores(g, n_kv - 1, nxt)
            m = consume(n_kv - 2, cur, m)
            if more:
                scores(g + 1, 0, cur)
            consume(n_kv - 1, nxt, m)
        finalize(g)


def _diff_attn(dq, dk, dv, lam_vec, subln_g, lam_init, q_start, n_q, kv_len):
    tq = _tile(n_q, 256, 8)
    tk = _tile(kv_len, 768, 128)
    n_grp = _tile(n_q // tq, 4, 1)
    q_off = q_start // tq
    assert q_off * tq == q_start
    q_specs = [pl.BlockSpec((tq, HEAD_DIM), functools.partial(lambda h, i, g: (n_grp * i + g + q_off, h), g=g))
               for g in range(n_grp)]
    return pl.pallas_call(
        functools.partial(_diff_attn_kernel, n_grp=n_grp, tk=tk, n_kv=kv_len // tk, lam_init=lam_init),
        out_shape=jax.ShapeDtypeStruct((n_q, W_DIFF), MXU_DTYPE),
        grid=(N_HEADS_DIFF, n_q // (tq * n_grp)),
        in_specs=[pl.BlockSpec((4, DIFF_DIM), lambda h, i: (0, 0)),
                  pl.BlockSpec((1, HEAD_DIM), lambda h, i: (0, 0)),
                  *q_specs,
                  pl.BlockSpec((kv_len, HEAD_DIM), lambda h, i: (0, h)),
                  pl.BlockSpec((kv_len, HEAD_DIM), lambda h, i: (0, h))],
        out_specs=pl.BlockSpec((tq * n_grp, HEAD_DIM), lambda h, i: (i, h)),
        scratch_shapes=[pltpu.VMEM((n_grp, 2 * tq, HEAD_DIM), MXU_DTYPE),
                        pltpu.VMEM((2 * tq, tk), F32), pltpu.VMEM((2 * tq, tk), F32),
                        pltpu.VMEM((2 * tq, 2 * HEAD_DIM), F32)],
        compiler_params=_params(2), name="diff_attention",
    )(lam_vec, subln_g.reshape(1, HEAD_DIM), *([dq] * n_grp), dk, dv)


_CAND_ROWS = PEER_TOPK + 8 * 7 + 8


def _top16_rows(x, out_ref):
    cur = x
    for r in range(PEER_TOPK):
        m = jnp.max(cur, axis=0, keepdims=True)
        out_ref[r:r + 1, :] = m
        if r + 1 < PEER_TOPK:
            cur = jnp.where(cur == m, -jnp.inf, cur)


def _route_kernel(q_ref, keys_ref, a_ref, b_ref, ea_ref, eb_ref, tau_ref, a16_ref, b16_ref, cand_ref):
    tt = q_ref.shape[0]
    sub = lax.broadcasted_iota(jnp.int32, (8, tt), 0)
    for h in range(PEER_HEADS):
        half = PEER_DQ // 2
        qa = q_ref[:, (2 * h) * half:(2 * h + 1) * half]
        qb = q_ref[:, (2 * h + 1) * half:(2 * h + 2) * half]
        a = lax.dot_general(keys_ref[2 * h], qa, NT_DIMS, preferred_element_type=F32)
        b = lax.dot_general(keys_ref[2 * h + 1], qb, NT_DIMS, preferred_element_type=F32)
        _top16_rows(a, a16_ref)
        _top16_rows(b, b16_ref)
        cand_ref[0:PEER_TOPK, :] = a16_ref[0:1, :] + b16_ref[...]
        for i in range(1, 8):
            c = a16_ref[i:i + 1, :] + b16_ref[0:8, :]
            cand_ref[8 + 8 * i:16 + 8 * i, :] = jnp.where(sub < PEER_TOPK // (i + 1), c, -jnp.inf)
        cand_ref[_CAND_ROWS - 8:_CAND_ROWS, :] = a16_ref[8:16, :] + b16_ref[0:1, :]
        cand = cand_ref[...]
        cur = cand
        for r in range(PEER_TOPK):
            tau = jnp.max(cur, axis=0, keepdims=True)
            if r + 1 < PEER_TOPK:
                cur = jnp.where(cur == tau, -jnp.inf, cur)
        a_max, b_max = a16_ref[0:1, :], b16_ref[0:1, :]
        z = jnp.sum(jnp.where(cand >= tau, jnp.exp(cand - (a_max + b_max)), 0.0), axis=0, keepdims=True)
        a_ref[h] = a
        b_ref[h] = b
        ea_ref[h] = jnp.exp(a - a_max)
        eb_ref[h] = jnp.exp(b - b_max) / z
        tau_ref[h] = tau


def _route(q, keys):
    t = q.shape[0]
    tt = _tile(t, 256, 128)
    grid_out = jax.ShapeDtypeStruct((PEER_HEADS, PEER_NKEYS, t), F32)
    big = pl.BlockSpec((PEER_HEADS, PEER_NKEYS, tt), lambda i: (0, 0, i))
    return pl.pallas_call(
        _route_kernel,
        out_shape=(grid_out, grid_out, grid_out, grid_out, jax.ShapeDtypeStruct((PEER_HEADS, 1, t), F32)),
        grid=(t // tt,),
        in_specs=[pl.BlockSpec((tt, PEER_HEADS * PEER_DQ), lambda i: (i, 0)),
                  pl.BlockSpec((2 * PEER_HEADS, PEER_NKEYS, PEER_DQ // 2), lambda i: (0, 0, 0))],
        out_specs=(big, big, big, big, pl.BlockSpec((PEER_HEADS, 1, tt), lambda i: (0, 0, i))),
        scratch_shapes=[pltpu.VMEM((PEER_TOPK, tt), F32), pltpu.VMEM((PEER_TOPK, tt), F32),
                        pltpu.VMEM((_CAND_ROWS, tt), F32)],
        compiler_params=_params(1), name="peer_route",
    )(q, keys.reshape(2 * PEER_HEADS, PEER_NKEYS, PEER_DQ // 2).astype(MXU_DTYPE))


def _mix_kernel(h_ref, u_ref, a_ref, ea_ref, b_ref, eb_ref, tau_ref, o_ref, gt_ref):
    act = jax.nn.gelu(lax.dot_general(h_ref[...], u_ref[...], NT_DIMS, preferred_element_type=F32))
    n_i = a_ref.shape[1]
    for ii in range(n_i):
        acc = None
        for h in range(PEER_HEADS):
            picked = (a_ref[h, ii:ii + 1, :] + b_ref[h]) >= tau_ref[h]
            g = jnp.where(picked, ea_ref[h, ii:ii + 1, :] * eb_ref[h], 0.0)
            acc = g if acc is None else acc + g
        gt_ref[ii * PEER_NKEYS:(ii + 1) * PEER_NKEYS, :] = acc
    o_ref[...] = (act * gt_ref[...].T).astype(o_ref.dtype)


def _mix(hf, u_tab, a, ea, b, eb, tau):
    t, d = hf.shape
    n_exp = u_tab.shape[0]
    tt = _tile(t, 768, 128)
    te = _tile(n_exp, 512, 8 * PEER_NKEYS)
    n_i = te // PEER_NKEYS
    part = pl.BlockSpec((PEER_HEADS, n_i, tt), lambda i, e: (0, e, i))
    full = pl.BlockSpec((PEER_HEADS, PEER_NKEYS, tt), lambda i, e: (0, 0, i))
    return pl.pallas_call(
        _mix_kernel,
        out_shape=jax.ShapeDtypeStruct((t, n_exp), MXU_DTYPE),
        grid=(t // tt, n_exp // te),
        in_specs=[pl.BlockSpec((tt, d), lambda i, e: (i, 0)),
                  pl.BlockSpec((te, d), lambda i, e: (e, 0)),
                  part, part, full, full,
                  pl.BlockSpec((PEER_HEADS, 1, tt), lambda i, e: (0, 0, i))],
        out_specs=pl.BlockSpec((tt, te), lambda i, e: (i, e)),
        scratch_shapes=[pltpu.VMEM((te, tt), F32)],
        compiler_params=_params(2), name="peer_mix",
    )(hf, u_tab, a, ea, b, eb, tau)


def _final_kernel(x_ref, g_ref, o_ref):
    x = x_ref[...]
    o_ref[...] = x * lax.rsqrt(jnp.mean(x * x, axis=-1, keepdims=True) + EPS) * g_ref[...]


def _final_norm(xs, g, n_ctx):
    t, d = xs.shape
    seq = t - n_ctx
    tm = _tile(math.gcd(seq, n_ctx), 256, 8)
    off = n_ctx // tm
    return pl.pallas_call(
        _final_kernel,
        out_shape=jax.ShapeDtypeStruct((seq, d), F32),
        grid=(seq // tm,),
        in_specs=[pl.BlockSpec((tm, d), lambda i: (i + off, 0)), pl.BlockSpec((1, d), lambda i: (0, 0))],
        out_specs=pl.BlockSpec((tm, d), lambda i: (i, 0)),
        compiler_params=_params(1), name="final_norm",
    )(xs, g.reshape(1, d))


def kernel(x, c, ctx, c_ctx, w_ada, b_ada, norm_g, w_in, w_out, mlp_ln_g, mlp_ln_b, mlp_w_s, mlp_b_s,
           ret_decay, diff_lam, diff_subln_g, peer_w_q, peer_keys, peer_u, peer_v, final_g):
    batch, seq, d = x.shape
    n_ctx = ctx.shape[1]
    depth = w_in.shape[0]
    assert batch == 1 and seq % CHUNK == 0 and n_ctx % CHUNK == 0 and seq % GRID_W == 0
    t = n_ctx + seq

    xs = jnp.concatenate([ctx[0], x[0]], axis=0)
    cond = jnp.zeros((MOD_ROWS, d), F32).at[0].set(c_ctx).at[1].set(c[0])
    mod_all = _ada(cond, w_ada, b_ada)
    tables = _rope_tables(n_ctx, seq)

    for l in range(depth):
        ctx_out = l < depth - 1
        lam_init = 0.8 - 0.6 * math.exp(-0.3 * l)
        mod = mod_all[l]

        h = _modulate(xs, norm_g[l, 0], mod, 0, n_ctx)
        w_in_l = w_in[l].astype(MXU_DTYPE)
        p_mlp = _matmul(h, w_in_l, 0, IN_MLP, F32)
        p_ret = _matmul(h, w_in_l, IN_MLP, IN_RET, F32)
        p_dif = _matmul(h, w_in_l, IN_MLP + IN_RET, IN_DIFF, F32)

        y_mlp = _gating(p_mlp, mlp_ln_g[l], mlp_ln_b[l], mlp_w_s[l], mlp_b_s[l])
        rq, rk, rv, dq, dk, dv = _prep(p_ret, p_dif, tables)
        y_fwd = _retention(rq, rk, rv, p_ret, ret_decay[l, 0], n_ctx, False, None, F32)
        y_ret = _retention(rq, rk, rv, p_ret, ret_decay[l, 1], n_ctx, True, y_fwd, MXU_DTYPE)
        y_dif = _diff_attn(dq, dk, dv, diff_lam[l], diff_subln_g[l], lam_init, n_ctx, seq, t)
        if ctx_out:
            y_dif_ctx = _diff_attn(dq, dk, dv, diff_lam[l], diff_subln_g[l], lam_init, 0, n_ctx, n_ctx)
        else:
            y_dif_ctx = jnp.zeros((n_ctx, W_DIFF), MXU_DTYPE)
        y_parts = [y_mlp, y_ret, jnp.concatenate([y_dif_ctx, y_dif], axis=0)]
        w_out_l = w_out[l]
        w_parts = [w_out_l[:W_MLP].astype(MXU_DTYPE), w_out_l[W_MLP:W_MLP + W_RET].astype(MXU_DTYPE),
                   w_out_l[W_MLP + W_RET:].astype(MXU_DTYPE)]
        xs = _matmul_residual(y_parts, w_parts, xs, mod, 2, n_ctx)

        hf = _modulate(xs, norm_g[l, 1], mod, 3, n_ctx)
        q = _matmul(hf, peer_w_q[l].astype(MXU_DTYPE), 0, PEER_HEADS * PEER_DQ, MXU_DTYPE)
        a, b, ea, eb, tau = _route(q, peer_keys[l])
        mix = _mix(hf, peer_u[l].astype(MXU_DTYPE), a, ea, b, eb, tau)
        xs = _matmul_residual([mix], [peer_v[l].astype(MXU_DTYPE)], xs, mod, 5, n_ctx)

    return _final_norm(xs, final_g, n_ctx)[None]
```

```python
import functools
import math

import jax
import jax.numpy as jnp
from jax import lax
from jax.experimental import pallas as pl
from jax.experimental.pallas import tpu as pltpu

F32 = jnp.float32
MXU_DTYPE = jnp.bfloat16

HEAD_DIM = 128
CHUNK = 128
GRID_W = 64
N_HEADS_MLP = 8
N_HEADS_RET = 12
N_HEADS_DIFF = 12
W_MLP = N_HEADS_MLP * HEAD_DIM
W_RET = N_HEADS_RET * HEAD_DIM
W_DIFF = N_HEADS_DIFF * HEAD_DIM
DIFF_DIM = HEAD_DIM // 2
IN_MLP = 2 * W_MLP
IN_RET = 5 * W_RET
IN_DIFF = 3 * W_DIFF
ROPE_BASE = 10000.0
PEER_HEADS = 8
PEER_NKEYS = 64
PEER_DQ = 256
PEER_TOPK = 16
EPS = 1e-6
LOG2_E = math.log2(math.e)

VMEM_LIMIT_V7X = 56 * 1024 * 1024
MOD_ROWS = 8

NT_DIMS = (((1,), (1,)), ((), ()))


def _params(n_axes):
    return pltpu.CompilerParams(dimension_semantics=("arbitrary",) * n_axes,
                                vmem_limit_bytes=VMEM_LIMIT_V7X)


def _tile(n, cap, mult):
    best = None
    for t in range(mult, min(n, cap) + 1, mult):
        if n % t == 0:
            best = t
    assert best is not None, (n, cap, mult)
    return best


def _row_is_ctx(tm, n_ctx, axis=0):
    row = pl.program_id(axis) * tm + lax.broadcasted_iota(jnp.int32, (tm, 1), 0)
    return row < n_ctx


def _ada_kernel(c_ref, w_ref, b_ref, o_ref):
    s = jax.nn.silu(c_ref[...]).astype(MXU_DTYPE)
    o_ref[0] = jnp.dot(s, w_ref[0].astype(MXU_DTYPE), preferred_element_type=F32) + b_ref[0]


def _ada(cond, w_ada, b_ada):
    depth, d, n = w_ada.shape
    tn = _tile(n, 512, 128)
    return pl.pallas_call(
        _ada_kernel,
        out_shape=jax.ShapeDtypeStruct((depth, MOD_ROWS, n), F32),
        grid=(depth, n // tn),
        in_specs=[pl.BlockSpec((MOD_ROWS, d), lambda l, j: (0, 0)),
                  pl.BlockSpec((1, d, tn), lambda l, j: (l, 0, j)),
                  pl.BlockSpec((1, 1, tn), lambda l, j: (l, 0, j))],
        out_specs=pl.BlockSpec((1, MOD_ROWS, tn), lambda l, j: (l, 0, j)),
        compiler_params=_params(2), name="ada",
    )(cond, w_ada, b_ada.reshape(depth, 1, n))


def _modulate_kernel(x_ref, g_ref, sh_ref, sc_ref, o_ref, *, n_ctx):
    x = x_ref[...]
    y = x * lax.rsqrt(jnp.mean(x * x, axis=-1, keepdims=True) + EPS) * g_ref[...]
    is_ctx = _row_is_ctx(x.shape[0], n_ctx)
    shift = jnp.where(is_ctx, sh_ref[0:1, :], sh_ref[1:2, :])
    scale = jnp.where(is_ctx, sc_ref[0:1, :], sc_ref[1:2, :])
    o_ref[...] = (y * (1.0 + scale) + shift).astype(o_ref.dtype)


def _modulate(xs, g, mod, shift_blk, n_ctx):
    t, d = xs.shape
    tm = _tile(t, 256, 8)
    return pl.pallas_call(
        functools.partial(_modulate_kernel, n_ctx=n_ctx),
        out_shape=jax.ShapeDtypeStruct((t, d), MXU_DTYPE),
        grid=(t // tm,),
        in_specs=[pl.BlockSpec((tm, d), lambda i: (i, 0)),
                  pl.BlockSpec((1, d), lambda i: (0, 0)),
                  pl.BlockSpec((MOD_ROWS, d), lambda i: (0, shift_blk)),
                  pl.BlockSpec((MOD_ROWS, d), lambda i: (0, shift_blk + 1))],
        out_specs=pl.BlockSpec((tm, d), lambda i: (i, 0)),
        compiler_params=_params(1), name="modulate",
    )(xs, g.reshape(1, d), mod, mod)


def _cast_weight_once(w_ref, wb_ref):
    @pl.when(pl.program_id(1) == 0)
    def _():
        wb_ref[...] = w_ref[...].astype(wb_ref.dtype)


def _mm_kernel(a_ref, w_ref, o_ref, wb_ref):
    _cast_weight_once(w_ref, wb_ref)
    o_ref[...] = jnp.dot(a_ref[...], wb_ref[...], preferred_element_type=F32).astype(o_ref.dtype)


def _mm_res_kernel(*refs, n_parts, n_ctx):
    a_refs = refs[:n_parts]
    w_ref, r_ref, g_ref, o_ref, wb_ref = refs[n_parts:]
    _cast_weight_once(w_ref, wb_ref)
    acc, row = None, 0
    for a_ref in a_refs:
        k = a_ref.shape[1]
        part = jnp.dot(a_ref[...], wb_ref[row:row + k, :], preferred_element_type=F32)
        acc = part if acc is None else acc + part
        row += k
    gate = jnp.where(_row_is_ctx(acc.shape[0], n_ctx, axis=1), g_ref[0:1, :], g_ref[1:2, :])
    o_ref[...] = r_ref[...] + gate * acc


def _mm_tiles(t, n, with_residual=False):
    return _tile(t, 704 if with_residual else 1408, 64), _tile(n, 512, 128)


def _matmul(a, w, layer, n_off, n, out_dtype):
    t, k = a.shape
    tm, tn = _mm_tiles(t, n)
    off = n_off // tn
    assert off * tn == n_off
    return pl.pallas_call(
        _mm_kernel,
        out_shape=jax.ShapeDtypeStruct((t, n), out_dtype),
        grid=(n // tn, t // tm),
        in_specs=[pl.BlockSpec((tm, k), lambda j, i: (i, 0)),
                  pl.BlockSpec((None, k, tn), lambda j, i: (layer, 0, j + off))],
        out_specs=pl.BlockSpec((tm, tn), lambda j, i: (i, j)),
        scratch_shapes=[pltpu.VMEM((k, tn), MXU_DTYPE)],
        compiler_params=_params(2), name="proj",
    )(a, w)


def _matmul_residual(a_parts, w, layer, res, mod, gate_blk, n_ctx):
    t, n = res.shape
    k = w.shape[1]
    assert sum(a.shape[1] for a in a_parts) == k
    tm, tn = _mm_tiles(t, n, with_residual=True)
    blocks_per_gate = n // tn
    a_specs = [pl.BlockSpec((tm, a.shape[1]), lambda j, i: (i, 0)) for a in a_parts]
    return pl.pallas_call(
        functools.partial(_mm_res_kernel, n_parts=len(a_parts), n_ctx=n_ctx),
        out_shape=jax.ShapeDtypeStruct((t, n), F32),
        grid=(n // tn, t // tm),
        in_specs=[*a_specs,
                  pl.BlockSpec((None, k, tn), lambda j, i: (layer, 0, j)),
                  pl.BlockSpec((tm, tn), lambda j, i: (i, j)),
                  pl.BlockSpec((MOD_ROWS, tn), lambda j, i: (0, gate_blk * blocks_per_gate + j))],
        out_specs=pl.BlockSpec((tm, tn), lambda j, i: (i, j)),
        scratch_shapes=[pltpu.VMEM((k, tn), MXU_DTYPE)],
        compiler_params=_params(2), name="proj_residual",
    )(*a_parts, w, res, mod)


def _prep_kernel(rq_ref, rk_ref, rv_ref, dq_ref, dk_ref, dv_ref, rc_ref, rs_ref, dc_ref, ds_ref,
                 orq_ref, ork_ref, orv_ref, odq_ref, odk_ref, odv_ref):
    rc, rs, dc, ds = rc_ref[...], rs_ref[...], dc_ref[...], ds_ref[...]
    lane = lax.broadcasted_iota(jnp.int32, rc.shape, 1)
    low16 = (lane & 16) == 0

    def rot_ret(x):
        return x * rc + pltpu.roll(x, 64, 1) * rs

    def rot_axial(x):
        partner = jnp.where(low16, pltpu.roll(x, 112, 1), pltpu.roll(x, 16, 1))
        return x * dc + partner * ds

    for h in range(N_HEADS_RET):
        sl = slice(h * HEAD_DIM, (h + 1) * HEAD_DIM)
        orq_ref[:, sl] = rot_ret(rq_ref[:, sl]).astype(MXU_DTYPE)
        ork_ref[:, sl] = (rot_ret(rk_ref[:, sl]) * (HEAD_DIM ** -0.5)).astype(MXU_DTYPE)
    for h in range(N_HEADS_DIFF):
        sl = slice(h * HEAD_DIM, (h + 1) * HEAD_DIM)
        odq_ref[:, sl] = (rot_axial(dq_ref[:, sl]) * (DIFF_DIM ** -0.5 * LOG2_E)).astype(MXU_DTYPE)
        odk_ref[:, sl] = rot_axial(dk_ref[:, sl]).astype(MXU_DTYPE)
    orv_ref[...] = rv_ref[...].astype(MXU_DTYPE)
    odv_ref[...] = dv_ref[...].astype(MXU_DTYPE)


def _prep(p_ret, p_dif, tables):
    t = p_ret.shape[0]
    tr = _tile(t, 256, 8)
    wide = lambda blk: pl.BlockSpec((tr, W_RET), lambda i: (i, blk))
    tab = pl.BlockSpec((tr, HEAD_DIM), lambda i: (i, 0))
    out = jax.ShapeDtypeStruct((t, W_RET), MXU_DTYPE)
    return pl.pallas_call(
        _prep_kernel,
        out_shape=(out,) * 6,
        grid=(t // tr,),
        in_specs=[wide(0), wide(1), wide(2), wide(0), wide(1), wide(2), tab, tab, tab, tab],
        out_specs=(wide(0),) * 6,
        compiler_params=_params(1), name="rope_prep",
    )(p_ret, p_ret, p_ret, p_dif, p_dif, p_dif, *tables)


def _rope_tables(n_ctx, seq):
    half = HEAD_DIM // 2
    inv = ROPE_BASE ** (-jnp.arange(half, dtype=F32) / half)
    ang = jnp.arange(n_ctx + seq, dtype=F32)[:, None] * inv[None, :]
    cos, sin = jnp.cos(ang), jnp.sin(ang)
    ret_cos = jnp.concatenate([cos, cos], axis=1)
    ret_sin = jnp.concatenate([-sin, sin], axis=1)

    rows = seq // GRID_W
    row = jnp.repeat(jnp.arange(rows), GRID_W).astype(F32)
    col = jnp.tile(jnp.arange(GRID_W), rows).astype(F32)
    ahalf = DIFF_DIM // 4
    ainv = ROPE_BASE ** (-jnp.arange(ahalf, dtype=F32) / ahalf)

    def group(pos):
        a = pos[:, None] * ainv[None, :]
        c, s = jnp.cos(a), jnp.sin(a)
        return jnp.concatenate([c, c], axis=1), jnp.concatenate([-s, s], axis=1)

    rc_, rs_ = group(row)
    cc_, cs_ = group(col)
    dcos = jnp.concatenate([rc_, cc_, rc_, cc_], axis=1)
    dsin = jnp.concatenate([rs_, cs_, rs_, cs_], axis=1)
    dcos = jnp.concatenate([jnp.ones((n_ctx, HEAD_DIM), F32), dcos], axis=0)
    dsin = jnp.concatenate([jnp.zeros((n_ctx, HEAD_DIM), F32), dsin], axis=0)
    return ret_cos, ret_sin, dcos, dsin


def _gating_kernel(p_ref, g_ref, b_ref, ws_ref, bs_ref, o_ref):
    u = jax.nn.gelu(p_ref[:, :W_MLP])
    gv = jax.nn.gelu(p_ref[:, W_MLP:])
    d = gv - jnp.mean(gv, axis=-1, keepdims=True)
    v = d * lax.rsqrt(jnp.mean(d * d, axis=-1, keepdims=True) + EPS) * g_ref[...] + b_ref[...]
    vb = v.astype(MXU_DTYPE)
    for h in range(N_HEADS_MLP):
        sl = slice(h * HEAD_DIM, (h + 1) * HEAD_DIM)
        mixed = jnp.dot(ws_ref[h], vb[:, sl], preferred_element_type=F32) + bs_ref[:, sl]
        o_ref[:, sl] = (u[:, sl] * mixed).astype(o_ref.dtype)


def _gating(p_mlp, ln_g, ln_b, w_s, b_s):
    t = p_mlp.shape[0]
    bias = jnp.repeat(b_s.T, HEAD_DIM, axis=1)
    return pl.pallas_call(
        _gating_kernel,
        out_shape=jax.ShapeDtypeStruct((t, W_MLP), MXU_DTYPE),
        grid=(t // CHUNK,),
        in_specs=[pl.BlockSpec((CHUNK, IN_MLP), lambda i: (i, 0)),
                  pl.BlockSpec((1, W_MLP), lambda i: (0, 0)),
                  pl.BlockSpec((1, W_MLP), lambda i: (0, 0)),
                  pl.BlockSpec((N_HEADS_MLP, CHUNK, CHUNK), lambda i: (0, 0, 0)),
                  pl.BlockSpec((CHUNK, W_MLP), lambda i: (0, 0))],
        out_specs=pl.BlockSpec((CHUNK, W_MLP), lambda i: (i, 0)),
        compiler_params=_params(1), name="spatial_gating",
    )(p_mlp, ln_g.reshape(1, W_MLP), ln_b.reshape(1, W_MLP), w_s.astype(MXU_DTYPE), bias)


def _retention_kernel(q_ref, k_ref, v_ref, g_ref, dm_ref, qd_ref, kd_ref, cd_ref, *rest, has_prev):
    if has_prev:
        prev_ref, o_ref, s_ref = rest
    else:
        o_ref, s_ref = rest

    @pl.when(pl.program_id(0) == 0)
    def _():
        s_ref[...] = jnp.zeros_like(s_ref)

    heads = range(N_HEADS_RET)
    sl = [slice(h * HEAD_DIM, (h + 1) * HEAD_DIM) for h in heads]
    q = [q_ref[:, sl[h]] for h in heads]
    k = [k_ref[:, sl[h]] for h in heads]
    v = [v_ref[:, sl[h]] for h in heads]
    s = [s_ref[h] for h in heads]
    scores = [lax.dot_general(q[h], k[h], NT_DIMS, preferred_element_type=F32) * dm_ref[h] for h in heads]
    inter = [qd_ref[h] * jnp.dot(q[h], s[h].astype(MXU_DTYPE), preferred_element_type=F32) for h in heads]
    kw_t = [(k[h].astype(F32) * kd_ref[h]).T.astype(MXU_DTYPE) for h in heads]
    for h in heads:
        s_ref[h] = s[h] * cd_ref[h] + jnp.dot(kw_t[h], v[h], preferred_element_type=F32)
    o = [jnp.dot(scores[h].astype(MXU_DTYPE), v[h], preferred_element_type=F32) + inter[h] for h in heads]
    d = [o[h] - jnp.mean(o[h], axis=-1, keepdims=True) for h in heads]
    r = [lax.rsqrt(jnp.mean(d[h] * d[h], axis=-1, keepdims=True) + EPS) for h in heads]
    for h in heads:
        y = d[h] * r[h] * jax.nn.silu(g_ref[:, sl[h]])
        if has_prev:
            y = y + prev_ref[:, sl[h]]
        o_ref[:, sl[h]] = y.astype(o_ref.dtype)


def _retention_tables(log_omg_dir, backward):
    log_g = jnp.log1p(-jnp.exp(log_omg_dir.astype(F32)))
    i = jnp.arange(CHUNK, dtype=F32)
    rel = i[:, None] - i[None, :]
    lower = rel >= 0
    intra = jnp.where(lower[None], jnp.exp(jnp.where(lower, rel, 0.0)[None] * log_g[:, None, None]), 0.0)
    q_decay = jnp.exp((i[None, :] + 1.0) * log_g[:, None])
    k_decay = jnp.exp((CHUNK - 1.0 - i)[None, :] * log_g[:, None])
    chunk_decay = jnp.exp(CHUNK * log_g)
    if backward:
        intra = jnp.swapaxes(intra, 1, 2)
        q_decay = q_decay[:, ::-1]
        k_decay = k_decay[:, ::-1]
    full = (N_HEADS_RET, CHUNK, HEAD_DIM)
    return (intra, jnp.broadcast_to(q_decay[:, :, None], full), jnp.broadcast_to(k_decay[:, :, None], full),
            jnp.broadcast_to(chunk_decay[:, None, None], full))


def _retention(rq, rk, rv, p_ret, log_omg_dir, n_ctx, backward, prev, out_dtype):
    t = rq.shape[0]
    n_chunks, n_ctx_chunks = t // CHUNK, n_ctx // CHUNK
    if backward:
        def chunk(s):
            return jnp.where(s < n_ctx_chunks, n_ctx_chunks - 1 - s, n_chunks + n_ctx_chunks - 1 - s)
    else:
        def chunk(s):
            return s
    gate_blk = 4 if backward else 3
    tok = pl.BlockSpec((CHUNK, W_RET), lambda s: (chunk(s), 0))
    tab = pl.BlockSpec((N_HEADS_RET, CHUNK, HEAD_DIM), lambda s: (0, 0, 0))
    in_specs = [tok, tok, tok, pl.BlockSpec((CHUNK, W_RET), lambda s: (chunk(s), gate_blk)), tab, tab, tab, tab]
    args = [rq, rk, rv, p_ret, *_retention_tables(log_omg_dir, backward)]
    if prev is not None:
        in_specs.append(tok)
        args.append(prev)
    return pl.pallas_call(
        functools.partial(_retention_kernel, has_prev=prev is not None),
        out_shape=jax.ShapeDtypeStruct((t, W_RET), out_dtype),
        grid=(n_chunks,),
        in_specs=in_specs,
        out_specs=tok,
        scratch_shapes=[pltpu.VMEM((N_HEADS_RET, HEAD_DIM, HEAD_DIM), F32)],
        compiler_params=_params(1), name="retention",
    )(*args)


def _diff_attn_kernel(lam_ref, g_ref, *refs, n_grp, tk, n_kv, lam_init):
    q_refs = refs[:n_grp]
    k_ref, v_ref, o_ref, qs_ref, s0_ref, s1_ref, acc_ref = refs[n_grp:]
    tq = q_refs[0].shape[0]
    lane = lax.broadcasted_iota(jnp.int32, (tq, HEAD_DIM), 1)
    for g in range(n_grp):
        q = q_refs[g][...].astype(F32)
        qs_ref[g, 0:tq, :] = jnp.where(lane < DIFF_DIM, q, 0.0).astype(MXU_DTYPE)
        qs_ref[g, tq:2 * tq, :] = jnp.where(lane >= DIFF_DIM, q, 0.0).astype(MXU_DTYPE)
    ones_col = jnp.where(lax.broadcasted_iota(jnp.int32, (tk, HEAD_DIM), 1) == 0, 1.0, 0.0).astype(MXU_DTYPE)
    lf = lam_ref[...]
    lam = (jnp.exp(jnp.sum(lf[0:1] * lf[1:2], axis=-1, keepdims=True))
           - jnp.exp(jnp.sum(lf[2:3] * lf[3:4], axis=-1, keepdims=True)) + lam_init)

    def scores(g, j, s_ref):
        start = pl.multiple_of(j * tk, tk)
        s_ref[...] = lax.dot_general(qs_ref[g], k_ref[pl.ds(start, tk), :], NT_DIMS, preferred_element_type=F32)

    def consume(j, s_ref, m):
        s = s_ref[...]
        m_new = jnp.maximum(m, jnp.max(s, axis=-1, keepdims=True))
        p = jnp.exp2(s - m_new).astype(MXU_DTYPE)
        start = pl.multiple_of(j * tk, tk)
        v_ext = jnp.concatenate([v_ref[pl.ds(start, tk), :], ones_col], axis=1)
        acc_ref[...] = jnp.exp2(m - m_new) * acc_ref[...] + jnp.dot(p, v_ext, preferred_element_type=F32)
        return m_new

    def finalize(g):
        acc = acc_ref[...]
        o = acc[:, :HEAD_DIM] / acc[:, HEAD_DIM:HEAD_DIM + 1]
        d = o[:tq] - lam * o[tq:]
        y = d * lax.rsqrt(jnp.mean(d * d, axis=-1, keepdims=True) + EPS) * g_ref[...]
        o_ref[g * tq:(g + 1) * tq, :] = (y * (1.0 - lam_init)).astype(o_ref.dtype)

    n_pairs = (n_kv - 1) // 2
    cur, nxt = s0_ref, s1_ref
    scores(0, 0, cur)
    for g in range(n_grp):
        acc_ref[...] = jnp.zeros_like(acc_ref)

        def pair(jj, m, g=g, cur=cur, nxt=nxt):
            j = 2 * jj
            scores(g, j + 1, nxt)
            m = consume(j, cur, m)
            scores(g, j + 2, cur)
            return consume(j + 1, nxt, m)

        m = lax.fori_loop(0, n_pairs, pair, jnp.full((2 * tq, 1), -jnp.inf, F32))
        more = g + 1 < n_grp
        if n_kv % 2 == 1:
            if more:
                scores(g + 1, 0, nxt)
            consume(n_kv - 1, cur, m)
            cur, nxt = nxt, cur
        else:
            scores(g, n_kv - 1, nxt)
            m = consume(n_kv - 2, cur, m)
            if more:
                scores(g + 1, 0, cur)
            consume(n_kv - 1, nxt, m)
        finalize(g)


def _diff_attn(dq, dk, dv, lam_vec, subln_g, lam_init, q_start, n_q, kv_len):
    tq = _tile(n_q, 256, 8)
    tk = _tile(kv_len, 768, 128)
    n_grp = _tile(n_q // tq, 4, 1)
    q_off = q_start // tq
    assert q_off * tq == q_start
    q_specs = [pl.BlockSpec((tq, HEAD_DIM), functools.partial(lambda h, i, g: (n_grp * i + g + q_off, h), g=g))
               for g in range(n_grp)]
    return pl.pallas_call(
        functools.partial(_diff_attn_kernel, n_grp=n_grp, tk=tk, n_kv=kv_len // tk, lam_init=lam_init),
        out_shape=jax.ShapeDtypeStruct((n_q, W_DIFF), MXU_DTYPE),
        grid=(N_HEADS_DIFF, n_q // (tq * n_grp)),
        in_specs=[pl.BlockSpec((4, DIFF_DIM), lambda h, i: (0, 0)),
                  pl.BlockSpec((1, HEAD_DIM), lambda h, i: (0, 0)),
                  *q_specs,
                  pl.BlockSpec((kv_len, HEAD_DIM), lambda h, i: (0, h)),
                  pl.BlockSpec((kv_len, HEAD_DIM), lambda h, i: (0, h))],
        out_specs=pl.BlockSpec((tq * n_grp, HEAD_DIM), lambda h, i: (i, h)),
        scratch_shapes=[pltpu.VMEM((n_grp, 2 * tq, HEAD_DIM), MXU_DTYPE),
                        pltpu.VMEM((2 * tq, tk), F32), pltpu.VMEM((2 * tq, tk), F32),
                        pltpu.VMEM((2 * tq, 2 * HEAD_DIM), F32)],
        compiler_params=_params(2), name="diff_attention",
    )(lam_vec, subln_g.reshape(1, HEAD_DIM), *([dq] * n_grp), dk, dv)


_CAND_ROWS = PEER_TOPK + 8 * 7 + 8


def _top16_rows(x, out_ref):
    cur = x
    for r in range(PEER_TOPK):
        m = jnp.max(cur, axis=0, keepdims=True)
        out_ref[r:r + 1, :] = m
        if r + 1 < PEER_TOPK:
            cur = jnp.where(cur == m, -jnp.inf, cur)


def _route_kernel(q_ref, keys_ref, a_ref, b_ref, ea_ref, eb_ref, tau_ref, a16_ref, b16_ref, cand_ref):
    tt = q_ref.shape[0]
    sub = lax.broadcasted_iota(jnp.int32, (8, tt), 0)
    for h in range(PEER_HEADS):
        half = PEER_DQ // 2
        qa = q_ref[:, (2 * h) * half:(2 * h + 1) * half]
        qb = q_ref[:, (2 * h + 1) * half:(2 * h + 2) * half]
        a = lax.dot_general(keys_ref[2 * h], qa, NT_DIMS, preferred_element_type=F32)
        b = lax.dot_general(keys_ref[2 * h + 1], qb, NT_DIMS, preferred_element_type=F32)
        _top16_rows(a, a16_ref)
        _top16_rows(b, b16_ref)
        cand_ref[0:PEER_TOPK, :] = a16_ref[0:1, :] + b16_ref[...]
        for i in range(1, 8):
            c = a16_ref[i:i + 1, :] + b16_ref[0:8, :]
            cand_ref[8 + 8 * i:16 + 8 * i, :] = jnp.where(sub < PEER_TOPK // (i + 1), c, -jnp.inf)
        cand_ref[_CAND_ROWS - 8:_CAND_ROWS, :] = a16_ref[8:16, :] + b16_ref[0:1, :]
        cand = cand_ref[...]
        cur = cand
        for r in range(PEER_TOPK):
            tau = jnp.max(cur, axis=0, keepdims=True)
            if r + 1 < PEER_TOPK:
                cur = jnp.where(cur == tau, -jnp.inf, cur)
        a_max, b_max = a16_ref[0:1, :], b16_ref[0:1, :]
        z = jnp.sum(jnp.where(cand >= tau, jnp.exp(cand - (a_max + b_max)), 0.0), axis=0, keepdims=True)
        a_ref[h] = a
        b_ref[h] = b
        ea_ref[h] = jnp.exp(a - a_max)
        eb_ref[h] = jnp.exp(b - b_max) / z
        tau_ref[h] = tau


def _route(q, keys):
    t = q.shape[0]
    tt = _tile(t, 256, 128)
    grid_out = jax.ShapeDtypeStruct((PEER_HEADS, PEER_NKEYS, t), F32)
    big = pl.BlockSpec((PEER_HEADS, PEER_NKEYS, tt), lambda i: (0, 0, i))
    return pl.pallas_call(
        _route_kernel,
        out_shape=(grid_out, grid_out, grid_out, grid_out, jax.ShapeDtypeStruct((PEER_HEADS, 1, t), F32)),
        grid=(t // tt,),
        in_specs=[pl.BlockSpec((tt, PEER_HEADS * PEER_DQ), lambda i: (i, 0)),
                  pl.BlockSpec((2 * PEER_HEADS, PEER_NKEYS, PEER_DQ // 2), lambda i: (0, 0, 0))],
        out_specs=(big, big, big, big, pl.BlockSpec((PEER_HEADS, 1, tt), lambda i: (0, 0, i))),
        scratch_shapes=[pltpu.VMEM((PEER_TOPK, tt), F32), pltpu.VMEM((PEER_TOPK, tt), F32),
                        pltpu.VMEM((_CAND_ROWS, tt), F32)],
        compiler_params=_params(1), name="peer_route",
    )(q, keys.reshape(2 * PEER_HEADS, PEER_NKEYS, PEER_DQ // 2).astype(MXU_DTYPE))


def _mix_kernel(h_ref, u_ref, a_ref, ea_ref, b_ref, eb_ref, tau_ref, o_ref, gt_ref, ub_ref):
    _cast_weight_once(u_ref, ub_ref)
    act = jax.nn.gelu(lax.dot_general(h_ref[...], ub_ref[...], NT_DIMS, preferred_element_type=F32))
    n_i = a_ref.shape[1]
    for ii in range(n_i):
        acc = None
        for h in range(PEER_HEADS):
            picked = (a_ref[h, ii:ii + 1, :] + b_ref[h]) >= tau_ref[h]
            g = jnp.where(picked, ea_ref[h, ii:ii + 1, :] * eb_ref[h], 0.0)
            acc = g if acc is None else acc + g
        gt_ref[ii * PEER_NKEYS:(ii + 1) * PEER_NKEYS, :] = acc
    o_ref[...] = (act * gt_ref[...].T).astype(o_ref.dtype)


def _mix(hf, u_tab, layer, a, ea, b, eb, tau):
    t, d = hf.shape
    n_exp = u_tab.shape[1]
    tt = _tile(t, 768, 128)
    te = _tile(n_exp, 512, 8 * PEER_NKEYS)
    n_i = te // PEER_NKEYS
    part = pl.BlockSpec((PEER_HEADS, n_i, tt), lambda e, i: (0, e, i))
    full = pl.BlockSpec((PEER_HEADS, PEER_NKEYS, tt), lambda e, i: (0, 0, i))
    return pl.pallas_call(
        _mix_kernel,
        out_shape=jax.ShapeDtypeStruct((t, n_exp), MXU_DTYPE),
        grid=(n_exp // te, t // tt),
        in_specs=[pl.BlockSpec((tt, d), lambda e, i: (i, 0)),
                  pl.BlockSpec((None, te, d), lambda e, i: (layer, e, 0)),
                  part, part, full, full,
                  pl.BlockSpec((PEER_HEADS, 1, tt), lambda e, i: (0, 0, i))],
        out_specs=pl.BlockSpec((tt, te), lambda e, i: (i, e)),
        scratch_shapes=[pltpu.VMEM((te, tt), F32), pltpu.VMEM((te, d), MXU_DTYPE)],
        compiler_params=_params(2), name="peer_mix",
    )(hf, u_tab, a, ea, b, eb, tau)


def _final_kernel(x_ref, g_ref, o_ref):
    x = x_ref[...]
    o_ref[...] = x * lax.rsqrt(jnp.mean(x * x, axis=-1, keepdims=True) + EPS) * g_ref[...]


def _final_norm(xs, g, n_ctx):
    t, d = xs.shape
    seq = t - n_ctx
    tm = _tile(math.gcd(seq, n_ctx), 256, 8)
    off = n_ctx // tm
    return pl.pallas_call(
        _final_kernel,
        out_shape=jax.ShapeDtypeStruct((seq, d), F32),
        grid=(seq // tm,),
        in_specs=[pl.BlockSpec((tm, d), lambda i: (i + off, 0)), pl.BlockSpec((1, d), lambda i: (0, 0))],
        out_specs=pl.BlockSpec((tm, d), lambda i: (i, 0)),
        compiler_params=_params(1), name="final_norm",
    )(xs, g.reshape(1, d))


def kernel(x, c, ctx, c_ctx, w_ada, b_ada, norm_g, w_in, w_out, mlp_ln_g, mlp_ln_b, mlp_w_s, mlp_b_s,
           ret_decay, diff_lam, diff_subln_g, peer_w_q, peer_keys, peer_u, peer_v, final_g):
    batch, seq, d = x.shape
    n_ctx = ctx.shape[1]
    depth = w_in.shape[0]
    assert batch == 1 and seq % CHUNK == 0 and n_ctx % CHUNK == 0 and seq % GRID_W == 0
    t = n_ctx + seq

    xs = jnp.concatenate([ctx[0], x[0]], axis=0)
    cond = jnp.zeros((MOD_ROWS, d), F32).at[0].set(c_ctx).at[1].set(c[0])
    mod_all = _ada(cond, w_ada, b_ada)
    tables = _rope_tables(n_ctx, seq)

    for l in range(depth):
        ctx_out = l < depth - 1
        lam_init = 0.8 - 0.6 * math.exp(-0.3 * l)
        mod = mod_all[l]

        h = _modulate(xs, norm_g[l, 0], mod, 0, n_ctx)
        p_mlp = _matmul(h, w_in, l, 0, IN_MLP, F32)
        p_ret = _matmul(h, w_in, l, IN_MLP, IN_RET, F32)
        p_dif = _matmul(h, w_in, l, IN_MLP + IN_RET, IN_DIFF, F32)

        y_mlp = _gating(p_mlp, mlp_ln_g[l], mlp_ln_b[l], mlp_w_s[l], mlp_b_s[l])
        rq, rk, rv, dq, dk, dv = _prep(p_ret, p_dif, tables)
        y_fwd = _retention(rq, rk, rv, p_ret, ret_decay[l, 0], n_ctx, False, None, F32)
        y_ret = _retention(rq, rk, rv, p_ret, ret_decay[l, 1], n_ctx, True, y_fwd, MXU_DTYPE)
        y_dif = _diff_attn(dq, dk, dv, diff_lam[l], diff_subln_g[l], lam_init, n_ctx, seq, t)
        if ctx_out:
            y_dif_ctx = _diff_attn(dq, dk, dv, diff_lam[l], diff_subln_g[l], lam_init, 0, n_ctx, n_ctx)
        else:
            y_dif_ctx = jnp.zeros((n_ctx, W_DIFF), MXU_DTYPE)
        y_parts = [y_mlp, y_ret, jnp.concatenate([y_dif_ctx, y_dif], axis=0)]
        xs = _matmul_residual(y_parts, w_out, l, xs, mod, 2, n_ctx)

        hf = _modulate(xs, norm_g[l, 1], mod, 3, n_ctx)
        q = _matmul(hf, peer_w_q, l, 0, PEER_HEADS * PEER_DQ, MXU_DTYPE)
        a, b, ea, eb, tau = _route(q, peer_keys[l])
        mix = _mix(hf, peer_u, l, a, ea, b, eb, tau)
        xs = _matmul_residual([mix], peer_v, l, xs, mod, 5, n_ctx)

    return _final_norm(xs, final_g, n_ctx)[None]
```

```python
import functools
import math

import jax
import jax.numpy as jnp
from jax import lax
from jax.experimental import pallas as pl
from jax.experimental.pallas import tpu as pltpu

F32 = jnp.float32
MXU_DTYPE = jnp.bfloat16

HEAD_DIM = 128
CHUNK = 128
GRID_W = 64
N_HEADS_MLP = 8
N_HEADS_RET = 12
N_HEADS_DIFF = 12
W_MLP = N_HEADS_MLP * HEAD_DIM
W_RET = N_HEADS_RET * HEAD_DIM
W_DIFF = N_HEADS_DIFF * HEAD_DIM
DIFF_DIM = HEAD_DIM // 2
IN_MLP = 2 * W_MLP
IN_RET = 5 * W_RET
IN_DIFF = 3 * W_DIFF
ROPE_BASE = 10000.0
PEER_HEADS = 8
PEER_NKEYS = 64
PEER_DQ = 256
PEER_TOPK = 16
EPS = 1e-6
LOG2_E = math.log2(math.e)

VMEM_LIMIT_V7X = 56 * 1024 * 1024
MOD_ROWS = 8

NT_DIMS = (((1,), (1,)), ((), ()))


def _params(n_axes):
    return pltpu.CompilerParams(dimension_semantics=("arbitrary",) * n_axes,
                                vmem_limit_bytes=VMEM_LIMIT_V7X)


def _tile(n, cap, mult):
    best = None
    for t in range(mult, min(n, cap) + 1, mult):
        if n % t == 0:
            best = t
    assert best is not None, (n, cap, mult)
    return best


def _row_is_ctx(tm, n_ctx, axis=0):
    row = pl.program_id(axis) * tm + lax.broadcasted_iota(jnp.int32, (tm, 1), 0)
    return row < n_ctx


def _ada_kernel(c_ref, w_ref, b_ref, o_ref):
    s = jax.nn.silu(c_ref[...]).astype(MXU_DTYPE)
    o_ref[0] = jnp.dot(s, w_ref[0].astype(MXU_DTYPE), preferred_element_type=F32) + b_ref[0]


def _ada(cond, w_ada, b_ada):
    depth, d, n = w_ada.shape
    tn = _tile(n, 512, 128)
    return pl.pallas_call(
        _ada_kernel,
        out_shape=jax.ShapeDtypeStruct((depth, MOD_ROWS, n), F32),
        grid=(depth, n // tn),
        in_specs=[pl.BlockSpec((MOD_ROWS, d), lambda l, j: (0, 0)),
                  pl.BlockSpec((1, d, tn), lambda l, j: (l, 0, j)),
                  pl.BlockSpec((1, 1, tn), lambda l, j: (l, 0, j))],
        out_specs=pl.BlockSpec((1, MOD_ROWS, tn), lambda l, j: (l, 0, j)),
        compiler_params=_params(2), name="ada",
    )(cond, w_ada, b_ada.reshape(depth, 1, n))


def _modulate_kernel(x_ref, g_ref, sh_ref, sc_ref, o_ref, *, n_ctx):
    x = x_ref[...]
    y = x * lax.rsqrt(jnp.mean(x * x, axis=-1, keepdims=True) + EPS) * g_ref[...]
    is_ctx = _row_is_ctx(x.shape[0], n_ctx)
    shift = jnp.where(is_ctx, sh_ref[0:1, :], sh_ref[1:2, :])
    scale = jnp.where(is_ctx, sc_ref[0:1, :], sc_ref[1:2, :])
    o_ref[...] = (y * (1.0 + scale) + shift).astype(o_ref.dtype)


def _modulate(xs, g, mod, shift_blk, n_ctx):
    t, d = xs.shape
    tm = _tile(t, 256, 8)
    return pl.pallas_call(
        functools.partial(_modulate_kernel, n_ctx=n_ctx),
        out_shape=jax.ShapeDtypeStruct((t, d), MXU_DTYPE),
        grid=(t // tm,),
        in_specs=[pl.BlockSpec((tm, d), lambda i: (i, 0)),
                  pl.BlockSpec((1, d), lambda i: (0, 0)),
                  pl.BlockSpec((MOD_ROWS, d), lambda i: (0, shift_blk)),
                  pl.BlockSpec((MOD_ROWS, d), lambda i: (0, shift_blk + 1))],
        out_specs=pl.BlockSpec((tm, d), lambda i: (i, 0)),
        compiler_params=_params(1), name="modulate",
    )(xs, g.reshape(1, d), mod, mod)


def _cast_weight_once(w_ref, wb_ref):
    @pl.when(pl.program_id(1) == 0)
    def _():
        wb_ref[...] = w_ref[...].astype(wb_ref.dtype)


def _mm_kernel(a_ref, w_ref, o_ref, wb_ref):
    _cast_weight_once(w_ref, wb_ref)
    o_ref[...] = jnp.dot(a_ref[...], wb_ref[...], preferred_element_type=F32).astype(o_ref.dtype)


def _mm_res_kernel(*refs, n_parts, n_ctx):
    a_refs = refs[:n_parts]
    w_ref, r_ref, g_ref, o_ref, wb_ref = refs[n_parts:]
    _cast_weight_once(w_ref, wb_ref)
    acc, row = None, 0
    for a_ref in a_refs:
        k = a_ref.shape[1]
        part = jnp.dot(a_ref[...], wb_ref[row:row + k, :], preferred_element_type=F32)
        acc = part if acc is None else acc + part
        row += k
    gate = jnp.where(_row_is_ctx(acc.shape[0], n_ctx, axis=1), g_ref[0:1, :], g_ref[1:2, :])
    o_ref[...] = r_ref[...] + gate * acc


def _mm_tiles(t, n, with_residual=False):
    return _tile(t, 704 if with_residual else 1408, 64), _tile(n, 512, 128)


def _matmul(a, w, layer, n_off, n, out_dtype):
    t, k = a.shape
    tm, tn = _mm_tiles(t, n)
    off = n_off // tn
    assert off * tn == n_off
    return pl.pallas_call(
        _mm_kernel,
        out_shape=jax.ShapeDtypeStruct((t, n), out_dtype),
        grid=(n // tn, t // tm),
        in_specs=[pl.BlockSpec((tm, k), lambda j, i: (i, 0)),
                  pl.BlockSpec((None, k, tn), lambda j, i: (layer, 0, j + off))],
        out_specs=pl.BlockSpec((tm, tn), lambda j, i: (i, j)),
        scratch_shapes=[pltpu.VMEM((k, tn), MXU_DTYPE)],
        compiler_params=_params(2), name="proj",
    )(a, w)


def _matmul_residual(a_parts, w, layer, res, mod, gate_blk, n_ctx):
    t, n = res.shape
    k = w.shape[1]
    assert sum(a.shape[1] for a in a_parts) == k
    tm, tn = _mm_tiles(t, n, with_residual=True)
    blocks_per_gate = n // tn
    a_specs = [pl.BlockSpec((tm, a.shape[1]), lambda j, i: (i, 0)) for a in a_parts]
    return pl.pallas_call(
        functools.partial(_mm_res_kernel, n_parts=len(a_parts), n_ctx=n_ctx),
        out_shape=jax.ShapeDtypeStruct((t, n), F32),
        grid=(n // tn, t // tm),
        in_specs=[*a_specs,
                  pl.BlockSpec((None, k, tn), lambda j, i: (layer, 0, j)),
                  pl.BlockSpec((tm, tn), lambda j, i: (i, j)),
                  pl.BlockSpec((MOD_ROWS, tn), lambda j, i: (0, gate_blk * blocks_per_gate + j))],
        out_specs=pl.BlockSpec((tm, tn), lambda j, i: (i, j)),
        scratch_shapes=[pltpu.VMEM((k, tn), MXU_DTYPE)],
        compiler_params=_params(2), name="proj_residual",
    )(*a_parts, w, res, mod)


def _mm_rope_kernel(a_ref, w_ref, cos_ref, sin_ref, o_ref, wb_ref, *, axial, scale):
    _cast_weight_once(w_ref, wb_ref)
    acc = jnp.dot(a_ref[...], wb_ref[...], preferred_element_type=F32)
    cos, sin = cos_ref[...], sin_ref[...]
    lane = lax.broadcasted_iota(jnp.int32, cos.shape, 1)
    low16 = (lane & 16) == 0
    for h in range(acc.shape[1] // HEAD_DIM):
        sl = slice(h * HEAD_DIM, (h + 1) * HEAD_DIM)
        x = acc[:, sl]
        if axial:
            partner = jnp.where(low16, pltpu.roll(x, 112, 1), pltpu.roll(x, 16, 1))
        else:
            partner = pltpu.roll(x, 64, 1)
        y = x * cos + partner * sin
        if scale != 1.0:
            y = y * scale
        o_ref[:, sl] = y.astype(o_ref.dtype)


def _matmul_rope(a, w, layer, n_off, n, cos, sin, axial, scale):
    t, k = a.shape
    tm, tn = _mm_tiles(t, n)
    off = n_off // tn
    assert off * tn == n_off
    tab = pl.BlockSpec((tm, HEAD_DIM), lambda j, i: (i, 0))
    return pl.pallas_call(
        functools.partial(_mm_rope_kernel, axial=axial, scale=scale),
        out_shape=jax.ShapeDtypeStruct((t, n), MXU_DTYPE),
        grid=(n // tn, t // tm),
        in_specs=[pl.BlockSpec((tm, k), lambda j, i: (i, 0)),
                  pl.BlockSpec((None, k, tn), lambda j, i: (layer, 0, j + off)),
                  tab, tab],
        out_specs=pl.BlockSpec((tm, tn), lambda j, i: (i, j)),
        scratch_shapes=[pltpu.VMEM((k, tn), MXU_DTYPE)],
        compiler_params=_params(2), name="proj_rope",
    )(a, w, cos, sin)


def _rope_tables(n_ctx, seq):
    half = HEAD_DIM // 2
    inv = ROPE_BASE ** (-jnp.arange(half, dtype=F32) / half)
    ang = jnp.arange(n_ctx + seq, dtype=F32)[:, None] * inv[None, :]
    cos, sin = jnp.cos(ang), jnp.sin(ang)
    ret_cos = jnp.concatenate([cos, cos], axis=1)
    ret_sin = jnp.concatenate([-sin, sin], axis=1)

    rows = seq // GRID_W
    row = jnp.repeat(jnp.arange(rows), GRID_W).astype(F32)
    col = jnp.tile(jnp.arange(GRID_W), rows).astype(F32)
    ahalf = DIFF_DIM // 4
    ainv = ROPE_BASE ** (-jnp.arange(ahalf, dtype=F32) / ahalf)

    def group(pos):
        a = pos[:, None] * ainv[None, :]
        c, s = jnp.cos(a), jnp.sin(a)
        return jnp.concatenate([c, c], axis=1), jnp.concatenate([-s, s], axis=1)

    rc_, rs_ = group(row)
    cc_, cs_ = group(col)
    dcos = jnp.concatenate([rc_, cc_, rc_, cc_], axis=1)
    dsin = jnp.concatenate([rs_, cs_, rs_, cs_], axis=1)
    dcos = jnp.concatenate([jnp.ones((n_ctx, HEAD_DIM), F32), dcos], axis=0)
    dsin = jnp.concatenate([jnp.zeros((n_ctx, HEAD_DIM), F32), dsin], axis=0)
    return ret_cos, ret_sin, dcos, dsin


def _gating_kernel(p_ref, g_ref, b_ref, ws_ref, bs_ref, o_ref):
    u = jax.nn.gelu(p_ref[:, :W_MLP])
    gv = jax.nn.gelu(p_ref[:, W_MLP:])
    d = gv - jnp.mean(gv, axis=-1, keepdims=True)
    v = d * lax.rsqrt(jnp.mean(d * d, axis=-1, keepdims=True) + EPS) * g_ref[...] + b_ref[...]
    vb = v.astype(MXU_DTYPE)
    for h in range(N_HEADS_MLP):
        sl = slice(h * HEAD_DIM, (h + 1) * HEAD_DIM)
        mixed = jnp.dot(ws_ref[h], vb[:, sl], preferred_element_type=F32) + bs_ref[:, sl]
        o_ref[:, sl] = (u[:, sl] * mixed).astype(o_ref.dtype)


def _gating(p_mlp, ln_g, ln_b, w_s, b_s):
    t = p_mlp.shape[0]
    bias = jnp.repeat(b_s.T, HEAD_DIM, axis=1)
    return pl.pallas_call(
        _gating_kernel,
        out_shape=jax.ShapeDtypeStruct((t, W_MLP), MXU_DTYPE),
        grid=(t // CHUNK,),
        in_specs=[pl.BlockSpec((CHUNK, IN_MLP), lambda i: (i, 0)),
                  pl.BlockSpec((1, W_MLP), lambda i: (0, 0)),
                  pl.BlockSpec((1, W_MLP), lambda i: (0, 0)),
                  pl.BlockSpec((N_HEADS_MLP, CHUNK, CHUNK), lambda i: (0, 0, 0)),
                  pl.BlockSpec((CHUNK, W_MLP), lambda i: (0, 0))],
        out_specs=pl.BlockSpec((CHUNK, W_MLP), lambda i: (i, 0)),
        compiler_params=_params(1), name="spatial_gating",
    )(p_mlp, ln_g.reshape(1, W_MLP), ln_b.reshape(1, W_MLP), w_s.astype(MXU_DTYPE), bias)


def _retention_kernel(q_ref, k_ref, v_ref, g_ref, dm_ref, qd_ref, kd_ref, cd_ref, *rest, has_prev):
    if has_prev:
        prev_ref, o_ref, s_ref = rest
    else:
        o_ref, s_ref = rest

    @pl.when(pl.program_id(0) == 0)
    def _():
        s_ref[...] = jnp.zeros_like(s_ref)

    heads = range(N_HEADS_RET)
    sl = [slice(h * HEAD_DIM, (h + 1) * HEAD_DIM) for h in heads]
    q = [q_ref[:, sl[h]] for h in heads]
    k = [k_ref[:, sl[h]] for h in heads]
    v = [v_ref[:, sl[h]] for h in heads]
    s = [s_ref[h] for h in heads]
    scores = [lax.dot_general(q[h], k[h], NT_DIMS, preferred_element_type=F32) * dm_ref[h] for h in heads]
    inter = [qd_ref[h] * jnp.dot(q[h], s[h].astype(MXU_DTYPE), preferred_element_type=F32) for h in heads]
    kw_t = [(k[h].astype(F32) * kd_ref[h]).T.astype(MXU_DTYPE) for h in heads]
    for h in heads:
        s_ref[h] = s[h] * cd_ref[h] + jnp.dot(kw_t[h], v[h], preferred_element_type=F32)
    o = [jnp.dot(scores[h].astype(MXU_DTYPE), v[h], preferred_element_type=F32) + inter[h] for h in heads]
    d = [o[h] - jnp.mean(o[h], axis=-1, keepdims=True) for h in heads]
    r = [lax.rsqrt(jnp.mean(d[h] * d[h], axis=-1, keepdims=True) + EPS) for h in heads]
    for h in heads:
        y = d[h] * r[h] * jax.nn.silu(g_ref[:, sl[h]])
        if has_prev:
            y = y + prev_ref[:, sl[h]]
        o_ref[:, sl[h]] = y.astype(o_ref.dtype)


def _retention_tables(log_omg_dir, backward):
    log_g = jnp.log1p(-jnp.exp(log_omg_dir.astype(F32)))
    i = jnp.arange(CHUNK, dtype=F32)
    rel = i[:, None] - i[None, :]
    lower = rel >= 0
    intra = jnp.where(lower[None], jnp.exp(jnp.where(lower, rel, 0.0)[None] * log_g[:, None, None]), 0.0)
    q_decay = jnp.exp((i[None, :] + 1.0) * log_g[:, None])
    k_decay = jnp.exp((CHUNK - 1.0 - i)[None, :] * log_g[:, None])
    chunk_decay = jnp.exp(CHUNK * log_g)
    if backward:
        intra = jnp.swapaxes(intra, 1, 2)
        q_decay = q_decay[:, ::-1]
        k_decay = k_decay[:, ::-1]
    full = (N_HEADS_RET, CHUNK, HEAD_DIM)
    return (intra, jnp.broadcast_to(q_decay[:, :, None], full), jnp.broadcast_to(k_decay[:, :, None], full),
            jnp.broadcast_to(chunk_decay[:, None, None], full))


def _retention(rq, rk, rv, gates, log_omg_dir, n_ctx, backward, prev, out_dtype):
    t = rq.shape[0]
    n_chunks, n_ctx_chunks = t // CHUNK, n_ctx // CHUNK
    if backward:
        def chunk(s):
            return jnp.where(s < n_ctx_chunks, n_ctx_chunks - 1 - s, n_chunks + n_ctx_chunks - 1 - s)
    else:
        def chunk(s):
            return s
    gate_blk = 1 if backward else 0
    tok = pl.BlockSpec((CHUNK, W_RET), lambda s: (chunk(s), 0))
    tab = pl.BlockSpec((N_HEADS_RET, CHUNK, HEAD_DIM), lambda s: (0, 0, 0))
    in_specs = [tok, tok, tok, pl.BlockSpec((CHUNK, W_RET), lambda s: (chunk(s), gate_blk)), tab, tab, tab, tab]
    args = [rq, rk, rv, gates, *_retention_tables(log_omg_dir, backward)]
    if prev is not None:
        in_specs.append(tok)
        args.append(prev)
    return pl.pallas_call(
        functools.partial(_retention_kernel, has_prev=prev is not None),
        out_shape=jax.ShapeDtypeStruct((t, W_RET), out_dtype),
        grid=(n_chunks,),
        in_specs=in_specs,
        out_specs=tok,
        scratch_shapes=[pltpu.VMEM((N_HEADS_RET, HEAD_DIM, HEAD_DIM), F32)],
        compiler_params=_params(1), name="retention",
    )(*args)


def _diff_attn_kernel(lam_ref, g_ref, *refs, n_grp, tk, n_kv, lam_init):
    q_refs = refs[:n_grp]
    k_ref, v_ref, o_ref, qs_ref, s0_ref, s1_ref, acc_ref = refs[n_grp:]
    tq = q_refs[0].shape[0]
    lane = lax.broadcasted_iota(jnp.int32, (tq, HEAD_DIM), 1)
    for g in range(n_grp):
        q = q_refs[g][...].astype(F32)
        qs_ref[g, 0:tq, :] = jnp.where(lane < DIFF_DIM, q, 0.0).astype(MXU_DTYPE)
        qs_ref[g, tq:2 * tq, :] = jnp.where(lane >= DIFF_DIM, q, 0.0).astype(MXU_DTYPE)
    ones_col = jnp.where(lax.broadcasted_iota(jnp.int32, (tk, HEAD_DIM), 1) == 0, 1.0, 0.0).astype(MXU_DTYPE)
    lf = lam_ref[...]
    lam = (jnp.exp(jnp.sum(lf[0:1] * lf[1:2], axis=-1, keepdims=True))
           - jnp.exp(jnp.sum(lf[2:3] * lf[3:4], axis=-1, keepdims=True)) + lam_init)

    def scores(g, j, s_ref):
        start = pl.multiple_of(j * tk, tk)
        s_ref[...] = lax.dot_general(qs_ref[g], k_ref[pl.ds(start, tk), :], NT_DIMS, preferred_element_type=F32)

    def consume(j, s_ref, m):
        s = s_ref[...]
        m_new = jnp.maximum(m, jnp.max(s, axis=-1, keepdims=True))
        p = jnp.exp2(s - m_new).astype(MXU_DTYPE)
        start = pl.multiple_of(j * tk, tk)
        v_ext = jnp.concatenate([v_ref[pl.ds(start, tk), :], ones_col], axis=1)
        acc_ref[...] = jnp.exp2(m - m_new) * acc_ref[...] + jnp.dot(p, v_ext, preferred_element_type=F32)
        return m_new

    def finalize(g):
        acc = acc_ref[...]
        o = acc[:, :HEAD_DIM] / acc[:, HEAD_DIM:HEAD_DIM + 1]
        d = o[:tq] - lam * o[tq:]
        y = d * lax.rsqrt(jnp.mean(d * d, axis=-1, keepdims=True) + EPS) * g_ref[...]
        o_ref[g * tq:(g + 1) * tq, :] = (y * (1.0 - lam_init)).astype(o_ref.dtype)

    n_pairs = (n_kv - 1) // 2
    cur, nxt = s0_ref, s1_ref
    scores(0, 0, cur)
    for g in range(n_grp):
        acc_ref[...] = jnp.zeros_like(acc_ref)

        def pair(jj, m, g=g, cur=cur, nxt=nxt):
            j = 2 * jj
            scores(g, j + 1, nxt)
            m = consume(j, cur, m)
            scores(g, j + 2, cur)
            return consume(j + 1, nxt, m)

        m = lax.fori_loop(0, n_pairs, pair, jnp.full((2 * tq, 1), -jnp.inf, F32), unroll=True)
        more = g + 1 < n_grp
        if n_kv % 2 == 1:
            if more:
                scores(g + 1, 0, nxt)
            consume(n_kv - 1, cur, m)
            cur, nxt = nxt, cur
        else:
            scores(g, n_kv - 1, nxt)
            m = consume(n_kv - 2, cur, m)
            if more:
                scores(g + 1, 0, cur)
            consume(n_kv - 1, nxt, m)
        finalize(g)


def _diff_attn(dq, dk, dv, lam_vec, subln_g, lam_init, q_start, n_q, kv_len):
    tq = _tile(n_q, 256, 8)
    tk = _tile(kv_len, 768, 128)
    n_grp = _tile(n_q // tq, 4, 1)
    q_off = q_start // tq
    assert q_off * tq == q_start
    q_specs = [pl.BlockSpec((tq, HEAD_DIM), functools.partial(lambda h, i, g: (n_grp * i + g + q_off, h), g=g))
               for g in range(n_grp)]
    return pl.pallas_call(
        functools.partial(_diff_attn_kernel, n_grp=n_grp, tk=tk, n_kv=kv_len // tk, lam_init=lam_init),
        out_shape=jax.ShapeDtypeStruct((n_q, W_DIFF), MXU_DTYPE),
        grid=(N_HEADS_DIFF, n_q // (tq * n_grp)),
        in_specs=[pl.BlockSpec((4, DIFF_DIM), lambda h, i: (0, 0)),
                  pl.BlockSpec((1, HEAD_DIM), lambda h, i: (0, 0)),
                  *q_specs,
                  pl.BlockSpec((kv_len, HEAD_DIM), lambda h, i: (0, h)),
                  pl.BlockSpec((kv_len, HEAD_DIM), lambda h, i: (0, h))],
        out_specs=pl.BlockSpec((tq * n_grp, HEAD_DIM), lambda h, i: (i, h)),
        scratch_shapes=[pltpu.VMEM((n_grp, 2 * tq, HEAD_DIM), MXU_DTYPE),
                        pltpu.VMEM((2 * tq, tk), F32), pltpu.VMEM((2 * tq, tk), F32),
                        pltpu.VMEM((2 * tq, 2 * HEAD_DIM), F32)],
        compiler_params=_params(2), name="diff_attention",
    )(lam_vec, subln_g.reshape(1, HEAD_DIM), *([dq] * n_grp), dk, dv)


_CAND_ROWS = PEER_TOPK + 8 * 7 + 8


def _top16_rows(x, out_ref):
    cur = x
    for r in range(PEER_TOPK):
        m = jnp.max(cur, axis=0, keepdims=True)
        out_ref[r:r + 1, :] = m
        if r + 1 < PEER_TOPK:
            cur = jnp.where(cur == m, -jnp.inf, cur)


def _route_kernel(q_ref, keys_ref, a_ref, b_ref, ea_ref, eb_ref, tau_ref, a16_ref, b16_ref, cand_ref):
    tt = q_ref.shape[0]
    sub = lax.broadcasted_iota(jnp.int32, (8, tt), 0)
    for h in range(PEER_HEADS):
        half = PEER_DQ // 2
        qa = q_ref[:, (2 * h) * half:(2 * h + 1) * half]
        qb = q_ref[:, (2 * h + 1) * half:(2 * h + 2) * half]
        a = lax.dot_general(keys_ref[2 * h], qa, NT_DIMS, preferred_element_type=F32)
        b = lax.dot_general(keys_ref[2 * h + 1], qb, NT_DIMS, preferred_element_type=F32)
        _top16_rows(a, a16_ref)
        _top16_rows(b, b16_ref)
        cand_ref[0:PEER_TOPK, :] = a16_ref[0:1, :] + b16_ref[...]
        for i in range(1, 8):
            c = a16_ref[i:i + 1, :] + b16_ref[0:8, :]
            cand_ref[8 + 8 * i:16 + 8 * i, :] = jnp.where(sub < PEER_TOPK // (i + 1), c, -jnp.inf)
        cand_ref[_CAND_ROWS - 8:_CAND_ROWS, :] = a16_ref[8:16, :] + b16_ref[0:1, :]
        cand = cand_ref[...]
        cur = cand
        for r in range(PEER_TOPK):
            tau = jnp.max(cur, axis=0, keepdims=True)
            if r + 1 < PEER_TOPK:
                cur = jnp.where(cur == tau, -jnp.inf, cur)
        a_max, b_max = a16_ref[0:1, :], b16_ref[0:1, :]
        z = jnp.sum(jnp.where(cand >= tau, jnp.exp(cand - (a_max + b_max)), 0.0), axis=0, keepdims=True)
        a_ref[h] = a
        b_ref[h] = b
        ea_ref[h] = jnp.exp(a - a_max)
        eb_ref[h] = jnp.exp(b - b_max) / z
        tau_ref[h] = tau


def _route(q, keys):
    t = q.shape[0]
    tt = _tile(t, 256, 128)
    grid_out = jax.ShapeDtypeStruct((PEER_HEADS, PEER_NKEYS, t), F32)
    big = pl.BlockSpec((PEER_HEADS, PEER_NKEYS, tt), lambda i: (0, 0, i))
    return pl.pallas_call(
        _route_kernel,
        out_shape=(grid_out, grid_out, grid_out, grid_out, jax.ShapeDtypeStruct((PEER_HEADS, 1, t), F32)),
        grid=(t // tt,),
        in_specs=[pl.BlockSpec((tt, PEER_HEADS * PEER_DQ), lambda i: (i, 0)),
                  pl.BlockSpec((2 * PEER_HEADS, PEER_NKEYS, PEER_DQ // 2), lambda i: (0, 0, 0))],
        out_specs=(big, big, big, big, pl.BlockSpec((PEER_HEADS, 1, tt), lambda i: (0, 0, i))),
        scratch_shapes=[pltpu.VMEM((PEER_TOPK, tt), F32), pltpu.VMEM((PEER_TOPK, tt), F32),
                        pltpu.VMEM((_CAND_ROWS, tt), F32)],
        compiler_params=_params(1), name="peer_route",
    )(q, keys.reshape(2 * PEER_HEADS, PEER_NKEYS, PEER_DQ // 2).astype(MXU_DTYPE))


def _mix_kernel(h_ref, u_ref, a_ref, ea_ref, b_ref, eb_ref, tau_ref, o_ref, gt_ref, ub_ref):
    _cast_weight_once(u_ref, ub_ref)
    act = jax.nn.gelu(lax.dot_general(h_ref[...], ub_ref[...], NT_DIMS, preferred_element_type=F32))
    n_i = a_ref.shape[1]
    for ii in range(n_i):
        acc = None
        for h in range(PEER_HEADS):
            picked = (a_ref[h, ii:ii + 1, :] + b_ref[h]) >= tau_ref[h]
            g = jnp.where(picked, ea_ref[h, ii:ii + 1, :] * eb_ref[h], 0.0)
            acc = g if acc is None else acc + g
        gt_ref[ii * PEER_NKEYS:(ii + 1) * PEER_NKEYS, :] = acc
    o_ref[...] = (act * gt_ref[...].T).astype(o_ref.dtype)


def _mix(hf, u_tab, layer, a, ea, b, eb, tau):
    t, d = hf.shape
    n_exp = u_tab.shape[1]
    tt = _tile(t, 768, 128)
    te = _tile(n_exp, 512, 8 * PEER_NKEYS)
    n_i = te // PEER_NKEYS
    part = pl.BlockSpec((PEER_HEADS, n_i, tt), lambda e, i: (0, e, i))
    full = pl.BlockSpec((PEER_HEADS, PEER_NKEYS, tt), lambda e, i: (0, 0, i))
    return pl.pallas_call(
        _mix_kernel,
        out_shape=jax.ShapeDtypeStruct((t, n_exp), MXU_DTYPE),
        grid=(n_exp // te, t // tt),
        in_specs=[pl.BlockSpec((tt, d), lambda e, i: (i, 0)),
                  pl.BlockSpec((None, te, d), lambda e, i: (layer, e, 0)),
                  part, part, full, full,
                  pl.BlockSpec((PEER_HEADS, 1, tt), lambda e, i: (0, 0, i))],
        out_specs=pl.BlockSpec((tt, te), lambda e, i: (i, e)),
        scratch_shapes=[pltpu.VMEM((te, tt), F32), pltpu.VMEM((te, d), MXU_DTYPE)],
        compiler_params=_params(2), name="peer_mix",
    )(hf, u_tab, a, ea, b, eb, tau)


def _final_kernel(x_ref, g_ref, o_ref):
    x = x_ref[...]
    o_ref[...] = x * lax.rsqrt(jnp.mean(x * x, axis=-1, keepdims=True) + EPS) * g_ref[...]


def _final_norm(xs, g, n_ctx):
    t, d = xs.shape
    seq = t - n_ctx
    tm = _tile(math.gcd(seq, n_ctx), 256, 8)
    off = n_ctx // tm
    return pl.pallas_call(
        _final_kernel,
        out_shape=jax.ShapeDtypeStruct((seq, d), F32),
        grid=(seq // tm,),
        in_specs=[pl.BlockSpec((tm, d), lambda i: (i + off, 0)), pl.BlockSpec((1, d), lambda i: (0, 0))],
        out_specs=pl.BlockSpec((tm, d), lambda i: (i, 0)),
        compiler_params=_params(1), name="final_norm",
    )(xs, g.reshape(1, d))


def kernel(x, c, ctx, c_ctx, w_ada, b_ada, norm_g, w_in, w_out, mlp_ln_g, mlp_ln_b, mlp_w_s, mlp_b_s,
           ret_decay, diff_lam, diff_subln_g, peer_w_q, peer_keys, peer_u, peer_v, final_g):
    batch, seq, d = x.shape
    n_ctx = ctx.shape[1]
    depth = w_in.shape[0]
    assert batch == 1 and seq % CHUNK == 0 and n_ctx % CHUNK == 0 and seq % GRID_W == 0
    t = n_ctx + seq

    xs = jnp.concatenate([ctx[0], x[0]], axis=0)
    cond = jnp.zeros((MOD_ROWS, d), F32).at[0].set(c_ctx).at[1].set(c[0])
    mod_all = _ada(cond, w_ada, b_ada)
    ret_cos, ret_sin, dif_cos, dif_sin = _rope_tables(n_ctx, seq)

    for l in range(depth):
        ctx_out = l < depth - 1
        lam_init = 0.8 - 0.6 * math.exp(-0.3 * l)
        mod = mod_all[l]

        h = _modulate(xs, norm_g[l, 0], mod, 0, n_ctx)
        ret0, dif0 = IN_MLP, IN_MLP + IN_RET
        p_mlp = _matmul(h, w_in, l, 0, IN_MLP, F32)
        rq = _matmul_rope(h, w_in, l, ret0, W_RET, ret_cos, ret_sin, False, 1.0)
        rk = _matmul_rope(h, w_in, l, ret0 + W_RET, W_RET, ret_cos, ret_sin, False, HEAD_DIM ** -0.5)
        rv = _matmul(h, w_in, l, ret0 + 2 * W_RET, W_RET, MXU_DTYPE)
        gates = _matmul(h, w_in, l, ret0 + 3 * W_RET, 2 * W_RET, F32)
        dq = _matmul_rope(h, w_in, l, dif0, W_DIFF, dif_cos, dif_sin, True, DIFF_DIM ** -0.5 * LOG2_E)
        dk = _matmul_rope(h, w_in, l, dif0 + W_DIFF, W_DIFF, dif_cos, dif_sin, True, 1.0)
        dv = _matmul(h, w_in, l, dif0 + 2 * W_DIFF, W_DIFF, MXU_DTYPE)

        y_mlp = _gating(p_mlp, mlp_ln_g[l], mlp_ln_b[l], mlp_w_s[l], mlp_b_s[l])
        y_fwd = _retention(rq, rk, rv, gates, ret_decay[l, 0], n_ctx, False, None, F32)
        y_ret = _retention(rq, rk, rv, gates, ret_decay[l, 1], n_ctx, True, y_fwd, MXU_DTYPE)
        y_dif = _diff_attn(dq, dk, dv, diff_lam[l], diff_subln_g[l], lam_init, n_ctx, seq, t)
        if ctx_out:
            y_dif_ctx = _diff_attn(dq, dk, dv, diff_lam[l], diff_subln_g[l], lam_init, 0, n_ctx, n_ctx)
        else:
            y_dif_ctx = jnp.zeros((n_ctx, W_DIFF), MXU_DTYPE)
        y_parts = [y_mlp, y_ret, jnp.concatenate([y_dif_ctx, y_dif], axis=0)]
        xs = _matmul_residual(y_parts, w_out, l, xs, mod, 2, n_ctx)

        hf = _modulate(xs, norm_g[l, 1], mod, 3, n_ctx)
        q = _matmul(hf, peer_w_q, l, 0, PEER_HEADS * PEER_DQ, MXU_DTYPE)
        a, b, ea, eb, tau = _route(q, peer_keys[l])
        mix = _mix(hf, peer_u, l, a, ea, b, eb, tau)
        xs = _matmul_residual([mix], peer_v, l, xs, mod, 5, n_ctx)

    return _final_norm(xs, final_g, n_ctx)[None]
```

```python
import functools
import math

import jax
import jax.numpy as jnp
from jax import lax
from jax.experimental import pallas as pl
from jax.experimental.pallas import tpu as pltpu

F32 = jnp.float32
MXU_DTYPE = jnp.bfloat16

HEAD_DIM = 128
CHUNK = 128
GRID_W = 64
N_HEADS_MLP = 8
N_HEADS_RET = 12
N_HEADS_DIFF = 12
W_MLP = N_HEADS_MLP * HEAD_DIM
W_RET = N_HEADS_RET * HEAD_DIM
W_DIFF = N_HEADS_DIFF * HEAD_DIM
DIFF_DIM = HEAD_DIM // 2
IN_MLP = 2 * W_MLP
IN_RET = 5 * W_RET
IN_DIFF = 3 * W_DIFF
ROPE_BASE = 10000.0
PEER_HEADS = 8
PEER_NKEYS = 64
PEER_DQ = 256
PEER_TOPK = 16
EPS = 1e-6
LOG2_E = math.log2(math.e)

VMEM_LIMIT_V7X = 56 * 1024 * 1024
MOD_ROWS = 8

NT_DIMS = (((1,), (1,)), ((), ()))


def _params(n_axes):
    return pltpu.CompilerParams(dimension_semantics=("arbitrary",) * n_axes,
                                vmem_limit_bytes=VMEM_LIMIT_V7X)


def _tile(n, cap, mult):
    best = None
    for t in range(mult, min(n, cap) + 1, mult):
        if n % t == 0:
            best = t
    assert best is not None, (n, cap, mult)
    return best


def _row_is_ctx(tm, n_ctx, axis=0):
    row = pl.program_id(axis) * tm + lax.broadcasted_iota(jnp.int32, (tm, 1), 0)
    return row < n_ctx


def _ada_kernel(c_ref, w_ref, b_ref, o_ref):
    s = jax.nn.silu(c_ref[...]).astype(MXU_DTYPE)
    o_ref[0] = jnp.dot(s, w_ref[0].astype(MXU_DTYPE), preferred_element_type=F32) + b_ref[0]


def _ada(cond, w_ada, b_ada):
    depth, d, n = w_ada.shape
    tn = _tile(n, 512, 128)
    return pl.pallas_call(
        _ada_kernel,
        out_shape=jax.ShapeDtypeStruct((depth, MOD_ROWS, n), F32),
        grid=(depth, n // tn),
        in_specs=[pl.BlockSpec((MOD_ROWS, d), lambda l, j: (0, 0)),
                  pl.BlockSpec((1, d, tn), lambda l, j: (l, 0, j)),
                  pl.BlockSpec((1, 1, tn), lambda l, j: (l, 0, j))],
        out_specs=pl.BlockSpec((1, MOD_ROWS, tn), lambda l, j: (l, 0, j)),
        compiler_params=_params(2), name="ada",
    )(cond, w_ada, b_ada.reshape(depth, 1, n))


ROW_GROUP = 16
GROUPS_PER_TRIP = 4


def _modulate_kernel(x_ref, g_ref, sh_ref, sc_ref, o_ref, gain_ref, *, n_ctx):
    tm = x_ref.shape[0]
    gain_ref[...] = g_ref[...] * (1.0 + sc_ref[...])

    def groups(r, carry):
        starts = [pl.multiple_of((r * GROUPS_PER_TRIP + u) * ROW_GROUP, ROW_GROUP) for u in range(GROUPS_PER_TRIP)]
        inv = []
        for start in starts:
            x = x_ref[pl.ds(start, ROW_GROUP), :]
            inv.append(lax.rsqrt(jnp.mean(x * x, axis=-1, keepdims=True) + EPS))
        for start, inv_u in zip(starts, inv):
            mod_row = jnp.where(pl.program_id(0) * tm + start < n_ctx, 0, 1)
            y = x_ref[pl.ds(start, ROW_GROUP), :] * inv_u
            y = y * gain_ref[pl.ds(mod_row, 1), :] + sh_ref[pl.ds(mod_row, 1), :]
            o_ref[pl.ds(start, ROW_GROUP), :] = y.astype(o_ref.dtype)
        return carry

    lax.fori_loop(0, tm // (ROW_GROUP * GROUPS_PER_TRIP), groups, 0)


def _modulate(xs, g, mod, shift_blk, n_ctx):
    t, d = xs.shape
    assert n_ctx % ROW_GROUP == 0
    tm = _tile(t, 384, ROW_GROUP * GROUPS_PER_TRIP)
    return pl.pallas_call(
        functools.partial(_modulate_kernel, n_ctx=n_ctx),
        out_shape=jax.ShapeDtypeStruct((t, d), MXU_DTYPE),
        grid=(t // tm,),
        in_specs=[pl.BlockSpec((tm, d), lambda i: (i, 0)),
                  pl.BlockSpec((1, d), lambda i: (0, 0)),
                  pl.BlockSpec((MOD_ROWS, d), lambda i: (0, shift_blk)),
                  pl.BlockSpec((MOD_ROWS, d), lambda i: (0, shift_blk + 1))],
        out_specs=pl.BlockSpec((tm, d), lambda i: (i, 0)),
        scratch_shapes=[pltpu.VMEM((MOD_ROWS, d), F32)],
        compiler_params=_params(1), name="modulate",
    )(xs, g.reshape(1, d), mod, mod)


def _cast_weight_once(w_ref, wb_ref):
    @pl.when(pl.program_id(1) == 0)
    def _():
        wb_ref[...] = w_ref[...].astype(wb_ref.dtype)


def _mm_kernel(a_ref, w_ref, o_ref, wb_ref):
    _cast_weight_once(w_ref, wb_ref)
    o_ref[...] = jnp.dot(a_ref[...], wb_ref[...], preferred_element_type=F32).astype(o_ref.dtype)


def _mm_res_kernel(*refs, n_parts, n_ctx):
    a_refs = refs[:n_parts]
    w_ref, r_ref, g_ref, o_ref, wb_ref = refs[n_parts:]
    _cast_weight_once(w_ref, wb_ref)
    acc, row = None, 0
    for a_ref in a_refs:
        k = a_ref.shape[1]
        part = jnp.dot(a_ref[...], wb_ref[row:row + k, :], preferred_element_type=F32)
        acc = part if acc is None else acc + part
        row += k
    gate = jnp.where(_row_is_ctx(acc.shape[0], n_ctx, axis=1), g_ref[0:1, :], g_ref[1:2, :])
    o_ref[...] = r_ref[...] + gate * acc


def _mm_tiles(t, n, with_residual=False):
    return _tile(t, 704 if with_residual else 1408, 64), _tile(n, 512, 128)


def _matmul(a, w, layer, n_off, n, out_dtype):
    t, k = a.shape
    tm, tn = _mm_tiles(t, n)
    off = n_off // tn
    assert off * tn == n_off
    return pl.pallas_call(
        _mm_kernel,
        out_shape=jax.ShapeDtypeStruct((t, n), out_dtype),
        grid=(n // tn, t // tm),
        in_specs=[pl.BlockSpec((tm, k), lambda j, i: (i, 0)),
                  pl.BlockSpec((None, k, tn), lambda j, i: (layer, 0, j + off))],
        out_specs=pl.BlockSpec((tm, tn), lambda j, i: (i, j)),
        scratch_shapes=[pltpu.VMEM((k, tn), MXU_DTYPE)],
        compiler_params=_params(2), name="proj",
    )(a, w)


def _matmul_residual(a_parts, w, layer, res, mod, gate_blk, n_ctx):
    t, n = res.shape
    k = w.shape[1]
    assert sum(a.shape[1] for a in a_parts) == k
    tm, tn = _mm_tiles(t, n, with_residual=True)
    blocks_per_gate = n // tn
    a_specs = [pl.BlockSpec((tm, a.shape[1]), lambda j, i: (i, 0)) for a in a_parts]
    return pl.pallas_call(
        functools.partial(_mm_res_kernel, n_parts=len(a_parts), n_ctx=n_ctx),
        out_shape=jax.ShapeDtypeStruct((t, n), F32),
        grid=(n // tn, t // tm),
        in_specs=[*a_specs,
                  pl.BlockSpec((None, k, tn), lambda j, i: (layer, 0, j)),
                  pl.BlockSpec((tm, tn), lambda j, i: (i, j)),
                  pl.BlockSpec((MOD_ROWS, tn), lambda j, i: (0, gate_blk * blocks_per_gate + j))],
        out_specs=pl.BlockSpec((tm, tn), lambda j, i: (i, j)),
        scratch_shapes=[pltpu.VMEM((k, tn), MXU_DTYPE)],
        compiler_params=_params(2), name="proj_residual",
    )(*a_parts, w, res, mod)


def _mm_rope_kernel(a_ref, w_ref, cos_ref, sin_ref, o_ref, wb_ref, *, axial, scale):
    _cast_weight_once(w_ref, wb_ref)
    acc = jnp.dot(a_ref[...], wb_ref[...], preferred_element_type=F32)
    cos, sin = cos_ref[...], sin_ref[...]
    lane = lax.broadcasted_iota(jnp.int32, cos.shape, 1)
    low16 = (lane & 16) == 0
    for h in range(acc.shape[1] // HEAD_DIM):
        sl = slice(h * HEAD_DIM, (h + 1) * HEAD_DIM)
        x = acc[:, sl]
        if axial:
            partner = jnp.where(low16, pltpu.roll(x, 112, 1), pltpu.roll(x, 16, 1))
        else:
            partner = pltpu.roll(x, 64, 1)
        y = x * cos + partner * sin
        if scale != 1.0:
            y = y * scale
        o_ref[:, sl] = y.astype(o_ref.dtype)


def _matmul_rope(a, w, layer, n_off, n, cos, sin, axial, scale):
    t, k = a.shape
    tm, tn = _mm_tiles(t, n)
    off = n_off // tn
    assert off * tn == n_off
    tab = pl.BlockSpec((tm, HEAD_DIM), lambda j, i: (i, 0))
    return pl.pallas_call(
        functools.partial(_mm_rope_kernel, axial=axial, scale=scale),
        out_shape=jax.ShapeDtypeStruct((t, n), MXU_DTYPE),
        grid=(n // tn, t // tm),
        in_specs=[pl.BlockSpec((tm, k), lambda j, i: (i, 0)),
                  pl.BlockSpec((None, k, tn), lambda j, i: (layer, 0, j + off)),
                  tab, tab],
        out_specs=pl.BlockSpec((tm, tn), lambda j, i: (i, j)),
        scratch_shapes=[pltpu.VMEM((k, tn), MXU_DTYPE)],
        compiler_params=_params(2), name="proj_rope",
    )(a, w, cos, sin)


def _rope_tables(n_ctx, seq):
    half = HEAD_DIM // 2
    inv = ROPE_BASE ** (-jnp.arange(half, dtype=F32) / half)
    ang = jnp.arange(n_ctx + seq, dtype=F32)[:, None] * inv[None, :]
    cos, sin = jnp.cos(ang), jnp.sin(ang)
    ret_cos = jnp.concatenate([cos, cos], axis=1)
    ret_sin = jnp.concatenate([-sin, sin], axis=1)

    rows = seq // GRID_W
    row = jnp.repeat(jnp.arange(rows), GRID_W).astype(F32)
    col = jnp.tile(jnp.arange(GRID_W), rows).astype(F32)
    ahalf = DIFF_DIM // 4
    ainv = ROPE_BASE ** (-jnp.arange(ahalf, dtype=F32) / ahalf)

    def group(pos):
        a = pos[:, None] * ainv[None, :]
        c, s = jnp.cos(a), jnp.sin(a)
        return jnp.concatenate([c, c], axis=1), jnp.concatenate([-s, s], axis=1)

    rc_, rs_ = group(row)
    cc_, cs_ = group(col)
    dcos = jnp.concatenate([rc_, cc_, rc_, cc_], axis=1)
    dsin = jnp.concatenate([rs_, cs_, rs_, cs_], axis=1)
    dcos = jnp.concatenate([jnp.ones((n_ctx, HEAD_DIM), F32), dcos], axis=0)
    dsin = jnp.concatenate([jnp.zeros((n_ctx, HEAD_DIM), F32), dsin], axis=0)
    return ret_cos, ret_sin, dcos, dsin


def _gating_kernel(p_ref, g_ref, b_ref, ws_ref, bs_ref, o_ref):
    for c in range(p_ref.shape[0] // CHUNK):
        rows = slice(c * CHUNK, (c + 1) * CHUNK)
        u = jax.nn.gelu(p_ref[rows, :W_MLP])
        gv = jax.nn.gelu(p_ref[rows, W_MLP:])
        d = gv - jnp.mean(gv, axis=-1, keepdims=True)
        v = d * lax.rsqrt(jnp.mean(d * d, axis=-1, keepdims=True) + EPS) * g_ref[...] + b_ref[...]
        vb = v.astype(MXU_DTYPE)
        for h in range(N_HEADS_MLP):
            sl = slice(h * HEAD_DIM, (h + 1) * HEAD_DIM)
            mixed = jnp.dot(ws_ref[h], vb[:, sl], preferred_element_type=F32) + bs_ref[:, sl]
            o_ref[rows, sl] = (u[:, sl] * mixed).astype(o_ref.dtype)


def _gating(p_mlp, ln_g, ln_b, w_s, b_s):
    t = p_mlp.shape[0]
    tr = CHUNK * _tile(t // CHUNK, 6, 1)
    bias = jnp.repeat(b_s.T, HEAD_DIM, axis=1)
    return pl.pallas_call(
        _gating_kernel,
        out_shape=jax.ShapeDtypeStruct((t, W_MLP), MXU_DTYPE),
        grid=(t // tr,),
        in_specs=[pl.BlockSpec((tr, IN_MLP), lambda i: (i, 0)),
                  pl.BlockSpec((1, W_MLP), lambda i: (0, 0)),
                  pl.BlockSpec((1, W_MLP), lambda i: (0, 0)),
                  pl.BlockSpec((N_HEADS_MLP, CHUNK, CHUNK), lambda i: (0, 0, 0)),
                  pl.BlockSpec((CHUNK, W_MLP), lambda i: (0, 0))],
        out_specs=pl.BlockSpec((tr, W_MLP), lambda i: (i, 0)),
        compiler_params=_params(1), name="spatial_gating",
    )(p_mlp, ln_g.reshape(1, W_MLP), ln_b.reshape(1, W_MLP), w_s.astype(MXU_DTYPE), bias)


def _retention_kernel(q_ref, k_ref, v_ref, g_ref, dm_ref, qd_ref, kd_ref, cd_ref, *rest, has_prev, backward):
    if has_prev:
        prev_ref, o_ref, s_ref = rest
    else:
        o_ref, s_ref = rest

    @pl.when(pl.program_id(0) == 0)
    def _():
        s_ref[...] = jnp.zeros_like(s_ref)

    n_sub = q_ref.shape[0] // CHUNK
    units = [(c, h) for c in range(n_sub) for h in range(N_HEADS_RET)]

    def tile(ref, c, h):
        return ref[c * CHUNK:(c + 1) * CHUNK, h * HEAD_DIM:(h + 1) * HEAD_DIM]

    q = {u: tile(q_ref, *u) for u in units}
    k = {u: tile(k_ref, *u) for u in units}
    v = {u: tile(v_ref, *u) for u in units}
    scores = {u: lax.dot_general(q[u], k[u], NT_DIMS, preferred_element_type=F32) * dm_ref[u[1]] for u in units}
    kw_t = {u: (k[u].astype(F32) * kd_ref[u[1]]).T.astype(MXU_DTYPE) for u in units}
    kv = {u: jnp.dot(kw_t[u], v[u], preferred_element_type=F32) for u in units}
    inter = {}
    for h in range(N_HEADS_RET):
        s = s_ref[h]
        for c in (reversed(range(n_sub)) if backward else range(n_sub)):
            inter[c, h] = qd_ref[h] * jnp.dot(q[c, h], s.astype(MXU_DTYPE), preferred_element_type=F32)
            s = s * cd_ref[h] + kv[c, h]
        s_ref[h] = s
    o = {u: jnp.dot(scores[u].astype(MXU_DTYPE), v[u], preferred_element_type=F32) + inter[u] for u in units}
    d = {u: o[u] - jnp.mean(o[u], axis=-1, keepdims=True) for u in units}
    r = {u: lax.rsqrt(jnp.mean(d[u] * d[u], axis=-1, keepdims=True) + EPS) for u in units}
    for c, h in units:
        rows, cols = slice(c * CHUNK, (c + 1) * CHUNK), slice(h * HEAD_DIM, (h + 1) * HEAD_DIM)
        y = d[c, h] * r[c, h] * jax.nn.silu(g_ref[rows, cols])
        if has_prev:
            y = y + prev_ref[rows, cols]
        o_ref[rows, cols] = y.astype(o_ref.dtype)


def _retention_tables(log_omg_dir, backward):
    log_g = jnp.log1p(-jnp.exp(log_omg_dir.astype(F32)))
    i = jnp.arange(CHUNK, dtype=F32)
    rel = i[:, None] - i[None, :]
    lower = rel >= 0
    intra = jnp.where(lower[None], jnp.exp(jnp.where(lower, rel, 0.0)[None] * log_g[:, None, None]), 0.0)
    q_decay = jnp.exp((i[None, :] + 1.0) * log_g[:, None])
    k_decay = jnp.exp((CHUNK - 1.0 - i)[None, :] * log_g[:, None])
    chunk_decay = jnp.exp(CHUNK * log_g)
    if backward:
        intra = jnp.swapaxes(intra, 1, 2)
        q_decay = q_decay[:, ::-1]
        k_decay = k_decay[:, ::-1]
    full = (N_HEADS_RET, CHUNK, HEAD_DIM)
    return (intra, jnp.broadcast_to(q_decay[:, :, None], full), jnp.broadcast_to(k_decay[:, :, None], full),
            jnp.broadcast_to(chunk_decay[:, None, None], full))


def _retention(rq, rk, rv, gates, log_omg_dir, n_ctx, backward, prev, out_dtype):
    t = rq.shape[0]
    n_sub = 2 if (n_ctx // CHUNK) % 2 == 0 and ((t - n_ctx) // CHUNK) % 2 == 0 else 1
    tb = n_sub * CHUNK
    n_blocks, n_ctx_blocks = t // tb, n_ctx // tb
    if backward:
        def chunk(s):
            return jnp.where(s < n_ctx_blocks, n_ctx_blocks - 1 - s, n_blocks + n_ctx_blocks - 1 - s)
    else:
        def chunk(s):
            return s
    gate_blk = 1 if backward else 0
    tok = pl.BlockSpec((tb, W_RET), lambda s: (chunk(s), 0))
    tab = pl.BlockSpec((N_HEADS_RET, CHUNK, HEAD_DIM), lambda s: (0, 0, 0))
    in_specs = [tok, tok, tok, pl.BlockSpec((tb, W_RET), lambda s: (chunk(s), gate_blk)), tab, tab, tab, tab]
    args = [rq, rk, rv, gates, *_retention_tables(log_omg_dir, backward)]
    if prev is not None:
        in_specs.append(tok)
        args.append(prev)
    return pl.pallas_call(
        functools.partial(_retention_kernel, has_prev=prev is not None, backward=backward),
        out_shape=jax.ShapeDtypeStruct((t, W_RET), out_dtype),
        grid=(n_blocks,),
        in_specs=in_specs,
        out_specs=tok,
        scratch_shapes=[pltpu.VMEM((N_HEADS_RET, HEAD_DIM, HEAD_DIM), F32)],
        compiler_params=_params(1), name="retention",
    )(*args)


def _diff_attn_kernel(lam_ref, g_ref, *refs, n_grp, tk, n_kv, lam_init):
    q_refs = refs[:n_grp]
    k_ref, v_ref, o_ref, qs_ref, s0_ref, s1_ref, acc_ref = refs[n_grp:]
    tq = q_refs[0].shape[0]
    lane = lax.broadcasted_iota(jnp.int32, (tq, HEAD_DIM), 1)
    for g in range(n_grp):
        q = q_refs[g][...].astype(F32)
        qs_ref[g, 0:tq, :] = jnp.where(lane < DIFF_DIM, q, 0.0).astype(MXU_DTYPE)
        qs_ref[g, tq:2 * tq, :] = jnp.where(lane >= DIFF_DIM, q, 0.0).astype(MXU_DTYPE)
    ones_col = jnp.where(lax.broadcasted_iota(jnp.int32, (tk, HEAD_DIM), 1) == 0, 1.0, 0.0).astype(MXU_DTYPE)
    lf = lam_ref[...]
    lam = (jnp.exp(jnp.sum(lf[0:1] * lf[1:2], axis=-1, keepdims=True))
           - jnp.exp(jnp.sum(lf[2:3] * lf[3:4], axis=-1, keepdims=True)) + lam_init)

    def scores(g, j, s_ref):
        start = pl.multiple_of(j * tk, tk)
        s_ref[...] = lax.dot_general(qs_ref[g], k_ref[pl.ds(start, tk), :], NT_DIMS, preferred_element_type=F32)

    def consume(j, s_ref, m):
        s = s_ref[...]
        m_new = jnp.maximum(m, jnp.max(s, axis=-1, keepdims=True))
        p = jnp.exp2(s - m_new).astype(MXU_DTYPE)
        start = pl.multiple_of(j * tk, tk)
        v_ext = jnp.concatenate([v_ref[pl.ds(start, tk), :], ones_col], axis=1)
        acc_ref[...] = jnp.exp2(m - m_new) * acc_ref[...] + jnp.dot(p, v_ext, preferred_element_type=F32)
        return m_new

    def finalize(g):
        acc = acc_ref[...]
        o = acc[:, :HEAD_DIM] / acc[:, HEAD_DIM:HEAD_DIM + 1]
        d = o[:tq] - lam * o[tq:]
        y = d * lax.rsqrt(jnp.mean(d * d, axis=-1, keepdims=True) + EPS) * g_ref[...]
        o_ref[g * tq:(g + 1) * tq, :] = (y * (1.0 - lam_init)).astype(o_ref.dtype)

    n_pairs = (n_kv - 1) // 2
    cur, nxt = s0_ref, s1_ref
    scores(0, 0, cur)
    for g in range(n_grp):
        acc_ref[...] = jnp.zeros_like(acc_ref)

        def pair(jj, m, g=g, cur=cur, nxt=nxt):
            j = 2 * jj
            scores(g, j + 1, nxt)
            m = consume(j, cur, m)
            scores(g, j + 2, cur)
            return consume(j + 1, nxt, m)

        m = lax.fori_loop(0, n_pairs, pair, jnp.full((2 * tq, 1), -jnp.inf, F32), unroll=True)
        more = g + 1 < n_grp
        if n_kv % 2 == 1:
            if more:
                scores(g + 1, 0, nxt)
            consume(n_kv - 1, cur, m)
            cur, nxt = nxt, cur
        else:
            scores(g, n_kv - 1, nxt)
            m = consume(n_kv - 2, cur, m)
            if more:
                scores(g + 1, 0, cur)
            consume(n_kv - 1, nxt, m)
        finalize(g)


def _diff_attn(dq, dk, dv, lam_vec, subln_g, lam_init, q_start, n_q, kv_len):
    tq = _tile(n_q, 256, 8)
    tk = _tile(kv_len, 768, 128)
    n_grp = _tile(n_q // tq, 4, 1)
    q_off = q_start // tq
    assert q_off * tq == q_start
    q_specs = [pl.BlockSpec((tq, HEAD_DIM), functools.partial(lambda h, i, g: (n_grp * i + g + q_off, h), g=g))
               for g in range(n_grp)]
    return pl.pallas_call(
        functools.partial(_diff_attn_kernel, n_grp=n_grp, tk=tk, n_kv=kv_len // tk, lam_init=lam_init),
        out_shape=jax.ShapeDtypeStruct((n_q, W_DIFF), MXU_DTYPE),
        grid=(N_HEADS_DIFF, n_q // (tq * n_grp)),
        in_specs=[pl.BlockSpec((4, DIFF_DIM), lambda h, i: (0, 0)),
                  pl.BlockSpec((1, HEAD_DIM), lambda h, i: (0, 0)),
                  *q_specs,
                  pl.BlockSpec((kv_len, HEAD_DIM), lambda h, i: (0, h)),
                  pl.BlockSpec((kv_len, HEAD_DIM), lambda h, i: (0, h))],
        out_specs=pl.BlockSpec((tq * n_grp, HEAD_DIM), lambda h, i: (i, h)),
        scratch_shapes=[pltpu.VMEM((n_grp, 2 * tq, HEAD_DIM), MXU_DTYPE),
                        pltpu.VMEM((2 * tq, tk), F32), pltpu.VMEM((2 * tq, tk), F32),
                        pltpu.VMEM((2 * tq, 2 * HEAD_DIM), F32)],
        compiler_params=_params(2), name="diff_attention",
    )(lam_vec, subln_g.reshape(1, HEAD_DIM), *([dq] * n_grp), dk, dv)


_CAND_ROWS = PEER_TOPK + 8 * 7 + 8


def _top16_rows(x, out_ref):
    cur = x
    for r in range(PEER_TOPK):
        m = jnp.max(cur, axis=0, keepdims=True)
        out_ref[r:r + 1, :] = m
        if r + 1 < PEER_TOPK:
            cur = jnp.where(cur == m, -jnp.inf, cur)


def _route_kernel(q_ref, keys_ref, a_ref, b_ref, ea_ref, eb_ref, tau_ref, a16_ref, b16_ref, cand_ref):
    tt = q_ref.shape[0]
    sub = lax.broadcasted_iota(jnp.int32, (8, tt), 0)
    for h in range(PEER_HEADS):
        half = PEER_DQ // 2
        qa = q_ref[:, (2 * h) * half:(2 * h + 1) * half]
        qb = q_ref[:, (2 * h + 1) * half:(2 * h + 2) * half]
        a = lax.dot_general(keys_ref[2 * h], qa, NT_DIMS, preferred_element_type=F32)
        b = lax.dot_general(keys_ref[2 * h + 1], qb, NT_DIMS, preferred_element_type=F32)
        _top16_rows(a, a16_ref)
        _top16_rows(b, b16_ref)
        cand_ref[0:PEER_TOPK, :] = a16_ref[0:1, :] + b16_ref[...]
        for i in range(1, 8):
            c = a16_ref[i:i + 1, :] + b16_ref[0:8, :]
            cand_ref[8 + 8 * i:16 + 8 * i, :] = jnp.where(sub < PEER_TOPK // (i + 1), c, -jnp.inf)
        cand_ref[_CAND_ROWS - 8:_CAND_ROWS, :] = a16_ref[8:16, :] + b16_ref[0:1, :]
        cand = cand_ref[...]
        cur = cand
        for r in range(PEER_TOPK):
            tau = jnp.max(cur, axis=0, keepdims=True)
            if r + 1 < PEER_TOPK:
                cur = jnp.where(cur == tau, -jnp.inf, cur)
        a_max, b_max = a16_ref[0:1, :], b16_ref[0:1, :]
        z = jnp.sum(jnp.where(cand >= tau, jnp.exp(cand - (a_max + b_max)), 0.0), axis=0, keepdims=True)
        a_ref[h] = a
        b_ref[h] = b
        ea_ref[h] = jnp.exp(a - a_max)
        eb_ref[h] = jnp.exp(b - b_max) / z
        tau_ref[h] = tau


def _route(q, keys):
    t = q.shape[0]
    tt = _tile(t, 256, 128)
    grid_out = jax.ShapeDtypeStruct((PEER_HEADS, PEER_NKEYS, t), F32)
    big = pl.BlockSpec((PEER_HEADS, PEER_NKEYS, tt), lambda i: (0, 0, i))
    return pl.pallas_call(
        _route_kernel,
        out_shape=(grid_out, grid_out, grid_out, grid_out, jax.ShapeDtypeStruct((PEER_HEADS, 1, t), F32)),
        grid=(t // tt,),
        in_specs=[pl.BlockSpec((tt, PEER_HEADS * PEER_DQ), lambda i: (i, 0)),
                  pl.BlockSpec((2 * PEER_HEADS, PEER_NKEYS, PEER_DQ // 2), lambda i: (0, 0, 0))],
        out_specs=(big, big, big, big, pl.BlockSpec((PEER_HEADS, 1, tt), lambda i: (0, 0, i))),
        scratch_shapes=[pltpu.VMEM((PEER_TOPK, tt), F32), pltpu.VMEM((PEER_TOPK, tt), F32),
                        pltpu.VMEM((_CAND_ROWS, tt), F32)],
        compiler_params=_params(1), name="peer_route",
    )(q, keys.reshape(2 * PEER_HEADS, PEER_NKEYS, PEER_DQ // 2).astype(MXU_DTYPE))


def _mix_kernel(h_ref, u_ref, a_ref, ea_ref, b_ref, eb_ref, tau_ref, o_ref, gt_ref, ub_ref):
    _cast_weight_once(u_ref, ub_ref)
    act = jax.nn.gelu(lax.dot_general(h_ref[...], ub_ref[...], NT_DIMS, preferred_element_type=F32))
    n_i = a_ref.shape[1]
    for ii in range(n_i):
        acc = None
        for h in range(PEER_HEADS):
            picked = (a_ref[h, ii:ii + 1, :] + b_ref[h]) >= tau_ref[h]
            g = jnp.where(picked, ea_ref[h, ii:ii + 1, :] * eb_ref[h], 0.0)
            acc = g if acc is None else acc + g
        gt_ref[ii * PEER_NKEYS:(ii + 1) * PEER_NKEYS, :] = acc
    o_ref[...] = (act * gt_ref[...].T).astype(o_ref.dtype)


def _mix(hf, u_tab, layer, a, ea, b, eb, tau):
    t, d = hf.shape
    n_exp = u_tab.shape[1]
    tt = _tile(t, 768, 128)
    te = _tile(n_exp, 512, 8 * PEER_NKEYS)
    n_i = te // PEER_NKEYS
    part = pl.BlockSpec((PEER_HEADS, n_i, tt), lambda e, i: (0, e, i))
    full = pl.BlockSpec((PEER_HEADS, PEER_NKEYS, tt), lambda e, i: (0, 0, i))
    return pl.pallas_call(
        _mix_kernel,
        out_shape=jax.ShapeDtypeStruct((t, n_exp), MXU_DTYPE),
        grid=(n_exp // te, t // tt),
        in_specs=[pl.BlockSpec((tt, d), lambda e, i: (i, 0)),
                  pl.BlockSpec((None, te, d), lambda e, i: (layer, e, 0)),
                  part, part, full, full,
                  pl.BlockSpec((PEER_HEADS, 1, tt), lambda e, i: (0, 0, i))],
        out_specs=pl.BlockSpec((tt, te), lambda e, i: (i, e)),
        scratch_shapes=[pltpu.VMEM((te, tt), F32), pltpu.VMEM((te, d), MXU_DTYPE)],
        compiler_params=_params(2), name="peer_mix",
    )(hf, u_tab, a, ea, b, eb, tau)


def _final_kernel(x_ref, g_ref, o_ref):
    g = g_ref[...]

    def group(r, carry):
        rows = pl.ds(pl.multiple_of(r * ROW_GROUP, ROW_GROUP), ROW_GROUP)
        x = x_ref[rows, :]
        o_ref[rows, :] = x * lax.rsqrt(jnp.mean(x * x, axis=-1, keepdims=True) + EPS) * g
        return carry

    lax.fori_loop(0, x_ref.shape[0] // ROW_GROUP, group, 0)


def _final_norm(xs, g, n_ctx):
    t, d = xs.shape
    seq = t - n_ctx
    tm = _tile(math.gcd(seq, n_ctx), 256, ROW_GROUP)
    off = n_ctx // tm
    return pl.pallas_call(
        _final_kernel,
        out_shape=jax.ShapeDtypeStruct((seq, d), F32),
        grid=(seq // tm,),
        in_specs=[pl.BlockSpec((tm, d), lambda i: (i + off, 0)), pl.BlockSpec((1, d), lambda i: (0, 0))],
        out_specs=pl.BlockSpec((tm, d), lambda i: (i, 0)),
        compiler_params=_params(1), name="final_norm",
    )(xs, g.reshape(1, d))


def kernel(x, c, ctx, c_ctx, w_ada, b_ada, norm_g, w_in, w_out, mlp_ln_g, mlp_ln_b, mlp_w_s, mlp_b_s,
           ret_decay, diff_lam, diff_subln_g, peer_w_q, peer_keys, peer_u, peer_v, final_g):
    batch, seq, d = x.shape
    n_ctx = ctx.shape[1]
    depth = w_in.shape[0]
    assert batch == 1 and seq % CHUNK == 0 and n_ctx % CHUNK == 0 and seq % GRID_W == 0
    t = n_ctx + seq

    xs = jnp.concatenate([ctx[0], x[0]], axis=0)
    cond = jnp.zeros((MOD_ROWS, d), F32).at[0].set(c_ctx).at[1].set(c[0])
    mod_all = _ada(cond, w_ada, b_ada)
    ret_cos, ret_sin, dif_cos, dif_sin = _rope_tables(n_ctx, seq)

    for l in range(depth):
        ctx_out = l < depth - 1
        lam_init = 0.8 - 0.6 * math.exp(-0.3 * l)
        mod = mod_all[l]

        h = _modulate(xs, norm_g[l, 0], mod, 0, n_ctx)
        ret0, dif0 = IN_MLP, IN_MLP + IN_RET
        p_mlp = _matmul(h, w_in, l, 0, IN_MLP, F32)
        rq = _matmul_rope(h, w_in, l, ret0, W_RET, ret_cos, ret_sin, False, 1.0)
        rk = _matmul_rope(h, w_in, l, ret0 + W_RET, W_RET, ret_cos, ret_sin, False, HEAD_DIM ** -0.5)
        rv = _matmul(h, w_in, l, ret0 + 2 * W_RET, W_RET, MXU_DTYPE)
        gates = _matmul(h, w_in, l, ret0 + 3 * W_RET, 2 * W_RET, F32)
        dq = _matmul_rope(h, w_in, l, dif0, W_DIFF, dif_cos, dif_sin, True, DIFF_DIM ** -0.5 * LOG2_E)
        dk = _matmul_rope(h, w_in, l, dif0 + W_DIFF, W_DIFF, dif_cos, dif_sin, True, 1.0)
        dv = _matmul(h, w_in, l, dif0 + 2 * W_DIFF, W_DIFF, MXU_DTYPE)

        y_mlp = _gating(p_mlp, mlp_ln_g[l], mlp_ln_b[l], mlp_w_s[l], mlp_b_s[l])
        y_fwd = _retention(rq, rk, rv, gates, ret_decay[l, 0], n_ctx, False, None, F32)
        y_ret = _retention(rq, rk, rv, gates, ret_decay[l, 1], n_ctx, True, y_fwd, MXU_DTYPE)
        y_dif = _diff_attn(dq, dk, dv, diff_lam[l], diff_subln_g[l], lam_init, n_ctx, seq, t)
        if ctx_out:
            y_dif_ctx = _diff_attn(dq, dk, dv, diff_lam[l], diff_subln_g[l], lam_init, 0, n_ctx, n_ctx)
        else:
            y_dif_ctx = jnp.zeros((n_ctx, W_DIFF), MXU_DTYPE)
        y_parts = [y_mlp, y_ret, jnp.concatenate([y_dif_ctx, y_dif], axis=0)]
        xs = _matmul_residual(y_parts, w_out, l, xs, mod, 2, n_ctx)

        hf = _modulate(xs, norm_g[l, 1], mod, 3, n_ctx)
        q = _matmul(hf, peer_w_q, l, 0, PEER_HEADS * PEER_DQ, MXU_DTYPE)
        a, b, ea, eb, tau = _route(q, peer_keys[l])
        mix = _mix(hf, peer_u, l, a, ea, b, eb, tau)
        xs = _matmul_residual([mix], peer_v, l, xs, mod, 5, n_ctx)

    return _final_norm(xs, final_g, n_ctx)[None]
```
